```python
import math
import jax, jax.numpy as jnp
from jax import lax
import numpy as np

D_MODEL = 1024
BATCH = 4
SEQ = 8192
DEPTH = 4

N_A = DEPTH // 2
N_B = DEPTH - N_A

RET_HEADS = 4
RET_QK_DIM = D_MODEL // RET_HEADS
RET_V_DIM = 2 * RET_QK_DIM
RET_CHUNK = 128
RET_THETA = 10000.0
RET_IN = 2 * D_MODEL + 2 * (RET_HEADS * RET_V_DIM)

DIL_GROUPS = ((128, 1), (512, 4), (2048, 16))
N_GROUPS = len(DIL_GROUPS)
DIL_HEADS = 8
DIL_HEAD_DIM = D_MODEL // DIL_HEADS
ROT_DIMS = DIL_HEAD_DIM // 4
ROPE_THETA = 500000.0

D_FF = 4 * D_MODEL

ALPHA = (2.0 * DEPTH) ** 0.25
BETA = (8.0 * DEPTH) ** -0.25
LN_EPS = 1e-5
GN_EPS = 1e-6

kernel_name = "yoco_retention_dilated_attention_trunk"


def _layernorm(x, g, b):
    xf = x.astype(jnp.float32)
    mu = jnp.mean(xf, axis=-1, keepdims=True)
    var = jnp.mean(jnp.square(xf - mu), axis=-1, keepdims=True)
    y = (xf - mu) * lax.rsqrt(var + LN_EPS) * g.astype(jnp.float32) + b.astype(jnp.float32)
    return y.astype(x.dtype)


def _sq_relu_mlp(x, w_up, w_down):
    return jnp.square(jax.nn.relu(x @ w_up)) @ w_down


def _rotate_every_two(t):
    t1, t2 = t[..., ::2], t[..., 1::2]
    return jnp.stack((-t2, t1), axis=-1).reshape(t.shape)


def _xpos_rotation(t):
    S, dk = t.shape[1], t.shape[-1]
    angle = 1.0 / (RET_THETA ** jnp.linspace(0.0, 1.0, dk // 2, dtype=jnp.float32))
    angle = jnp.repeat(angle, 2)
    ang = jnp.arange(S, dtype=jnp.float32)[:, None] * angle[None]
    cos = jnp.cos(ang)[None, :, None, :].astype(t.dtype)
    sin = jnp.sin(ang)[None, :, None, :].astype(t.dtype)
    return t * cos + _rotate_every_two(t) * sin


def _chunkwise_retention(q, k, v):
    B, S, H, dk = q.shape
    dv = v.shape[-1]
    C = RET_CHUNK
    n = S // C
    dt = v.dtype
    log_g = jnp.log(1.0 - 2.0 ** (-5.0 - jnp.arange(H, dtype=jnp.float32)))
    idx = jnp.arange(C, dtype=jnp.float32)
    diff = idx[:, None] - idx[None, :]
    decay = jnp.where(diff[None] >= 0, jnp.exp(jnp.maximum(diff, 0.0)[None] * log_g[:, None, None]), 0.0).astype(dt)
    xi = jnp.exp((idx[None] + 1.0) * log_g[:, None]).astype(dt)
    zeta = jnp.exp((C - 1.0 - idx[None]) * log_g[:, None]).astype(dt)
    g_chunk = jnp.exp(C * log_g).astype(dt)

    def to_chunks(t):
        return t.reshape(B, n, C, H, t.shape[-1]).transpose(1, 0, 3, 2, 4)

    def step(R, inp):
        qc, kc, vc = inp
        s = jnp.einsum('bhid,bhjd->bhij', qc, kc) * decay
        o = jnp.einsum('bhij,bhjv->bhiv', s, vc) + jnp.einsum('bhid,bhdv->bhiv', qc, R) * xi[None, :, :, None]
        R = g_chunk[None, :, None, None] * R + jnp.einsum('bhjd,bhjv->bhdv', kc * zeta[None, :, :, None], vc)
        return R, o

    R0 = jnp.zeros((B, H, dk, dv), dt)
    _, o = lax.scan(step, R0, (to_chunks(q), to_chunks(k), to_chunks(v)))
    return o.transpose(1, 0, 3, 2, 4).reshape(B, S, H, dv)


def _retention_mixer(x, w_in, w_out):
    B, S, _ = x.shape
    proj = x @ w_in
    vw = RET_HEADS * RET_V_DIM
    q, k, v, g = jnp.split(proj, [D_MODEL, 2 * D_MODEL, 2 * D_MODEL + vw], axis=-1)
    q = _xpos_rotation(q.reshape(B, S, RET_HEADS, RET_QK_DIM))
    k = _xpos_rotation(k.reshape(B, S, RET_HEADS, RET_QK_DIM)) * (RET_QK_DIM ** -0.5)
    v = v.reshape(B, S, RET_HEADS, RET_V_DIM)
    o = _chunkwise_retention(q, k, v)
    of = o.astype(jnp.float32)
    of = of * lax.rsqrt(jnp.mean(jnp.square(of), axis=-1, keepdims=True) + GN_EPS)
    o = of.astype(x.dtype).reshape(B, S, vw)
    return (jax.nn.silu(g) * o) @ w_out


def _partial_rope(t):
    S = t.shape[1]
    half = ROT_DIMS // 2
    inv_freq = ROPE_THETA ** (-jnp.arange(0, ROT_DIMS, 2, dtype=jnp.float32) / ROT_DIMS)
    ang = jnp.arange(S, dtype=jnp.float32)[:, None] * inv_freq[None]
    shp = (1, S) + (1,) * (t.ndim - 3) + (half,)
    cos = jnp.cos(ang).reshape(shp).astype(t.dtype)
    sin = jnp.sin(ang).reshape(shp).astype(t.dtype)
    x1, x2, rest = t[..., :half], t[..., half:ROT_DIMS], t[..., ROT_DIMS:]
    return jnp.concatenate([x1 * cos - x2 * sin, x2 * cos + x1 * sin, rest], axis=-1)


def _shared_kv(x, kv_w):
    B, S, _ = x.shape
    kv = (x @ kv_w).reshape(B, S, 2, N_GROUPS, DIL_HEADS, DIL_HEAD_DIM)
    return _partial_rope(kv[:, :, 0]), kv[:, :, 1]


def _dilated_group(q, k, v, window, dilation):
    B, S, H, Dh = q.shape
    blk = window // dilation
    n_sub = -(-S // dilation)
    n_blk = -(-n_sub // blk)
    Lp = n_blk * blk * dilation
    pad = ((0, 0), (0, Lp - S), (0, 0), (0, 0))

    def to_blocks(t):
        return jnp.pad(t, pad).reshape(B, n_blk, blk, dilation, H, Dh)

    def with_prev(t):
        prev = jnp.pad(t, ((0, 0), (1, 0), (0, 0), (0, 0), (0, 0), (0, 0)))[:, :-1]
        return jnp.concatenate([prev, t], axis=2)

    qb = to_blocks(q)
    kc, vc = with_prev(to_blocks(k)), with_prev(to_blocks(v))
    s = jnp.einsum('bnirhd,bnjrhd->bnrhij', qb, kc).astype(jnp.float32) * (Dh ** -0.5)
    i = jnp.arange(blk)[:, None]
    j = jnp.arange(2 * blk)[None, :]
    dist = blk + i - j
    band = (dist >= 0) & (dist <= blk)
    mask = band[None] & ((jnp.arange(n_blk)[:, None, None] > 0) | (j[None] >= blk))
    s = jnp.where(mask[None, :, None, None], s, -jnp.inf)
    lse = jax.nn.logsumexp(s, axis=-1)
    p = jnp.exp(s - lse[..., None]).astype(v.dtype)
    o = jnp.einsum('bnrhij,bnjrhd->bnirhd', p, vc).reshape(B, Lp, H, Dh)[:, :S]
    lse = lse.transpose(0, 1, 4, 2, 3).reshape(B, Lp, H)[:, :S]
    return o, lse


def _dilated_mixer(x, w_q, w_out, k_sh, v_sh):
    B, S, _ = x.shape
    q = _partial_rope((x @ w_q).reshape(B, S, N_GROUPS, DIL_HEADS, DIL_HEAD_DIM))
    outs, lses = [], []
    for gi, (window, dilation) in enumerate(DIL_GROUPS):
        o, lse = _dilated_group(q[:, :, gi], k_sh[:, :, gi], v_sh[:, :, gi], window, dilation)
        outs.append(o)
        lses.append(lse)
    w = jax.nn.softmax(jnp.stack(lses, axis=0), axis=0).astype(x.dtype)
    o = jnp.sum(w[..., None] * jnp.stack(outs, axis=0), axis=0)
    return o.reshape(B, S, D_MODEL) @ w_out


def setup_inputs(seed: int = 0) -> dict:
    key = jax.random.key(seed)
    ks = jax.random.split(key, 10)

    def nrm(k, shape, fan_in, scale=1.0):
        return jax.random.normal(k, shape, jnp.float32) * (scale * fan_in ** -0.5)

    x = jax.random.normal(ks[0], (BATCH, SEQ, D_MODEL), jnp.float32)
    ret_w_in = nrm(ks[1], (N_A, D_MODEL, RET_IN), D_MODEL)
    ret_w_out = nrm(ks[2], (N_A, RET_HEADS * RET_V_DIM, D_MODEL), RET_HEADS * RET_V_DIM, BETA)
    kv_w = nrm(ks[3], (D_MODEL, 2 * N_GROUPS * D_MODEL), D_MODEL)
    dil_w_q = nrm(ks[4], (N_B, D_MODEL, N_GROUPS * D_MODEL), D_MODEL)
    dil_w_out = nrm(ks[5], (N_B, D_MODEL, D_MODEL), D_MODEL, BETA)
    mlp_w_up = nrm(ks[6], (DEPTH, D_MODEL, D_FF), D_MODEL)
    mlp_w_down = nrm(ks[7], (DEPTH, D_FF, D_MODEL), D_FF, BETA)
    ln_g = 1.0 + 0.02 * jax.random.normal(ks[8], (DEPTH, 2, D_MODEL), jnp.float32)
    ln_b = 0.02 * jax.random.normal(ks[9], (DEPTH, 2, D_MODEL), jnp.float32)
    return {"x": x, "ret_w_in": ret_w_in, "ret_w_out": ret_w_out, "kv_w": kv_w,
            "dil_w_q": dil_w_q, "dil_w_out": dil_w_out, "mlp_w_up": mlp_w_up,
            "mlp_w_down": mlp_w_down, "ln_g": ln_g, "ln_b": ln_b}


def reference(x, ret_w_in, ret_w_out, kv_w, dil_w_q, dil_w_out, mlp_w_up, mlp_w_down, ln_g, ln_b):
    k_sh, v_sh = None, None
    for l in range(DEPTH):
        if l < N_A:
            mix = _retention_mixer(x, ret_w_in[l], ret_w_out[l])
        else:
            mix = _dilated_mixer(x, dil_w_q[l - N_A], dil_w_out[l - N_A], k_sh, v_sh)
        x = _layernorm(ALPHA * x + mix, ln_g[l, 0], ln_b[l, 0])
        x = _layernorm(ALPHA * x + _sq_relu_mlp(x, mlp_w_up[l], mlp_w_down[l]), ln_g[l, 1], ln_b[l, 1])
        if l == N_A - 1:
            k_sh, v_sh = _shared_kv(x, kv_w)
    return x
```

```python
import functools

import jax
import jax.numpy as jnp
from jax import lax
from jax.experimental import pallas as pl
from jax.experimental.pallas import tpu as pltpu

D_MODEL = 1024
DEPTH = 4
N_A = DEPTH // 2

RET_HEADS = 4
RET_QK_DIM = D_MODEL // RET_HEADS
RET_V_DIM = 2 * RET_QK_DIM
RET_CHUNK = 128
RET_THETA = 10000.0
RET_VW = RET_HEADS * RET_V_DIM

DIL_GROUPS = ((128, 1), (512, 4), (2048, 16))
DILATIONS = tuple(d for _, d in DIL_GROUPS)
N_GROUPS = len(DIL_GROUPS)
DIL_HEADS = 8
DIL_HEAD_DIM = D_MODEL // DIL_HEADS
ROT_DIMS = DIL_HEAD_DIM // 4
ROPE_THETA = 500000.0
DIL_BLK = 128

D_FF = 4 * D_MODEL
ALPHA = (2.0 * DEPTH) ** 0.25
LN_EPS = 1e-5
GN_EPS = 1e-6

LANES = 128
N_LANE_BLOCKS = D_MODEL // LANES
VMEM_LIMIT = 56 * 1024 * 1024
MASKED = -1e30

BF16 = jnp.bfloat16
F32 = jnp.float32


def _params(*sem):
    return pltpu.CompilerParams(dimension_semantics=sem, vmem_limit_bytes=VMEM_LIMIT)


def _resident(shape):
    return pl.BlockSpec(shape, lambda *_: (0,) * len(shape))


def _layernorm(y, g, b):
    mu = jnp.mean(y, axis=-1, keepdims=True)
    yc = y - mu
    var = jnp.mean(yc * yc, axis=-1, keepdims=True)
    return yc * lax.rsqrt(var + LN_EPS) * g + b


def _ret_proj_kernel(x_ref, w_ref, c_ref, s_ref, o_ref, xb_ref, *, n_rot, k_from, k_scale):
    j = pl.program_id(1)

    @pl.when(j == 0)
    def _():
        xb_ref[...] = x_ref[...].astype(BF16)

    acc = jnp.dot(xb_ref[...], w_ref[...], preferred_element_type=F32)

    @pl.when(j < n_rot)
    def _():
        sc = jnp.where(j >= k_from, k_scale, 1.0).astype(F32)
        c = c_ref[...] * sc
        s = s_ref[...] * sc
        for h in range(acc.shape[1] // (2 * LANES)):
            ev = slice(2 * h * LANES, (2 * h + 1) * LANES)
            od = slice((2 * h + 1) * LANES, (2 * h + 2) * LANES)
            e, o = acc[:, ev], acc[:, od]
            o_ref[:, ev] = (e * c - o * s).astype(BF16)
            o_ref[:, od] = (o * c + e * s).astype(BF16)

    @pl.when(j >= n_rot)
    def _():
        o_ref[...] = acc.astype(BF16)


def _ret_project(x, w, cos, sin, *, seq, tm=1024, tn=512):
    m, kdim = x.shape
    n = w.shape[1]
    nseq = seq // tm
    kern = functools.partial(_ret_proj_kernel, n_rot=2 * D_MODEL // tn, k_from=D_MODEL // tn,
                             k_scale=RET_QK_DIM ** -0.5)
    return pl.pallas_call(
        kern,
        out_shape=jax.ShapeDtypeStruct((m, n), BF16),
        grid=(m // tm, n // tn),
        in_specs=[
            pl.BlockSpec((tm, kdim), lambda i, j: (i, 0)),
            pl.BlockSpec((kdim, tn), lambda i, j: (0, j)),
            pl.BlockSpec((tm, LANES), lambda i, j: (i % nseq, 0)),
            pl.BlockSpec((tm, LANES), lambda i, j: (i % nseq, 0)),
        ],
        out_specs=pl.BlockSpec((tm, tn), lambda i, j: (i, j)),
        scratch_shapes=[pltpu.VMEM((tm, kdim), BF16)],
        compiler_params=_params("parallel", "arbitrary"),
        name="ret_proj",
    )(x, w, cos, sin)


def _ret_kernel(q_ref, k_ref, v_ref, g_ref, dec_ref, xi_ref, zeta_ref, gc_ref, o_ref, r_ref):
    @pl.when(pl.program_id(2) == 0)
    def _():
        r_ref[...] = jnp.zeros_like(r_ref)

    c_sz = RET_CHUNK
    for c in range(q_ref.shape[0] // c_sz):
        rows = slice(c * c_sz, (c + 1) * c_sz)
        qc = q_ref[rows, :]
        kc = k_ref[rows, :]
        vc = v_ref[rows, :]
        s = lax.dot_general(qc, kc, (((1,), (1,)), ((), ())), preferred_element_type=F32) * dec_ref[...]
        r_old = r_ref[...]
        o = jnp.dot(s.astype(BF16), vc, preferred_element_type=F32)
        o = o + jnp.dot(qc, r_old.astype(BF16), preferred_element_type=F32) * xi_ref[...]
        kz = (kc.astype(F32) * zeta_ref[...]).astype(BF16)
        r_ref[...] = gc_ref[...] * r_old + lax.dot_general(
            kz, vc, (((0,), (0,)), ((), ())), preferred_element_type=F32)
        of = o * lax.rsqrt(jnp.mean(o * o, axis=-1, keepdims=True) + GN_EPS)
        gate = g_ref[rows, :].astype(F32)
        o_ref[rows, :] = (gate * jax.nn.sigmoid(gate) * of).astype(BF16)


def _retention(proj, dec, xi, zeta, gc, *, batch, seq, tc=512):
    ns = seq // tc
    qb = RET_QK_DIM
    vb = RET_V_DIM
    row = lambda b, i: b * ns + i
    return pl.pallas_call(
        _ret_kernel,
        out_shape=jax.ShapeDtypeStruct((batch * seq, RET_VW), BF16),
        grid=(batch, RET_HEADS, ns),
        in_specs=[
            pl.BlockSpec((tc, qb), lambda b, h, i: (row(b, i), h)),
            pl.BlockSpec((tc, qb), lambda b, h, i: (row(b, i), D_MODEL // qb + h)),
            pl.BlockSpec((tc, vb), lambda b, h, i: (row(b, i), 2 * D_MODEL // vb + h)),
            pl.BlockSpec((tc, vb), lambda b, h, i: (row(b, i), (2 * D_MODEL + RET_VW) // vb + h)),
            pl.BlockSpec((None, RET_CHUNK, RET_CHUNK), lambda b, h, i: (h, 0, 0)),
            pl.BlockSpec((None, RET_CHUNK, vb), lambda b, h, i: (h, 0, 0)),
            pl.BlockSpec((None, RET_CHUNK, qb), lambda b, h, i: (h, 0, 0)),
            pl.BlockSpec((None, 1, vb), lambda b, h, i: (h, 0, 0)),
        ],
        out_specs=pl.BlockSpec((tc, vb), lambda b, h, i: (row(b, i), h)),
        scratch_shapes=[pltpu.VMEM((qb, vb), F32)],
        compiler_params=_params("parallel", "parallel", "arbitrary"),
        name="retention",
    )(proj, proj, proj, proj, dec, xi, zeta, gc)


def _out_ln_kernel(a_ref, w_ref, x_ref, g_ref, b_ref, o_ref):
    y = ALPHA * x_ref[...] + jnp.dot(a_ref[...], w_ref[...], preferred_element_type=F32)
    o_ref[...] = _layernorm(y, g_ref[...], b_ref[...])


def _out_ln(a, w, x, g, b, *, tm=512):
    m, kdim = a.shape
    return pl.pallas_call(
        _out_ln_kernel,
        out_shape=jax.ShapeDtypeStruct((m, D_MODEL), F32),
        grid=(m // tm,),
        in_specs=[
            pl.BlockSpec((tm, kdim), lambda i: (i, 0)),
            _resident((kdim, D_MODEL)),
            pl.BlockSpec((tm, D_MODEL), lambda i: (i, 0)),
            _resident((1, D_MODEL)),
            _resident((1, D_MODEL)),
        ],
        out_specs=pl.BlockSpec((tm, D_MODEL), lambda i: (i, 0)),
        compiler_params=_params("parallel"),
        name="out_ln",
    )(a, w, x, g, b)


def _mlp_ln_kernel(x_ref, wu_ref, wd_ref, g_ref, b_ref, o_ref, *, ff_chunk):
    x = x_ref[...]
    xb = x.astype(BF16)
    y = ALPHA * x
    for c in range(wu_ref.shape[1] // ff_chunk):
        cols = slice(c * ff_chunk, (c + 1) * ff_chunk)
        h = jnp.maximum(jnp.dot(xb, wu_ref[:, cols], preferred_element_type=F32), 0.0)
        y = y + jnp.dot((h * h).astype(BF16), wd_ref[cols, :], preferred_element_type=F32)
    o_ref[...] = _layernorm(y, g_ref[...], b_ref[...])


def _mlp_ln(x, wu, wd, g, b, *, tm=512, ff_chunk=1024):
    m = x.shape[0]
    return pl.pallas_call(
        functools.partial(_mlp_ln_kernel, ff_chunk=ff_chunk),
        out_shape=jax.ShapeDtypeStruct((m, D_MODEL), F32),
        grid=(m // tm,),
        in_specs=[
            pl.BlockSpec((tm, D_MODEL), lambda i: (i, 0)),
            _resident((D_MODEL, D_FF)),
            _resident((D_FF, D_MODEL)),
            _resident((1, D_MODEL)),
            _resident((1, D_MODEL)),
        ],
        out_specs=pl.BlockSpec((tm, D_MODEL), lambda i: (i, 0)),
        compiler_params=_params("parallel"),
        name="mlp_ln",
    )(x, wu, wd, g, b)


def _dil_proj_kernel(*refs, units, tm):
    nb = N_LANE_BLOCKS
    x_refs = refs[:nb]
    w_ref = refs[nb]
    tabs = refs[nb + 1:nb + 1 + 2 * N_GROUPS]
    outs = refs[nb + 1 + 2 * N_GROUPS:-1]
    xb_ref = refs[-1]
    j = pl.program_id(1)

    @pl.when(j == 0)
    def _():
        for gi, d in enumerate(DILATIONS):
            n = tm // d
            for c in range(nb):
                cols = slice(c * LANES, (c + 1) * LANES)
                for r in range(d):
                    piece = x_refs[c][...] if d == 1 else x_refs[c][pl.ds(r, n, stride=d), :]
                    xb_ref[gi, r * n:(r + 1) * n, cols] = piece.astype(BF16)

    acc = jnp.dot(xb_ref[j % N_GROUPS], w_ref[...], preferred_element_type=F32)

    for jj, (rope, scale) in enumerate(units):
        gi = jj % N_GROUPS
        d = DILATIONS[gi]
        n = tm // d

        @pl.when(j == jj)
        def _():
            for h in range(DIL_HEADS):
                t = acc[:, h * LANES:(h + 1) * LANES]
                if rope:
                    t = t * tabs[2 * gi][...] + pltpu.roll(t, LANES // 2, axis=1) * tabs[2 * gi + 1][...]
                if scale != 1.0:
                    t = t * scale
                tb = t.astype(BF16)
                for r in range(d):
                    outs[jj][:, r * D_MODEL + h * LANES:r * D_MODEL + (h + 1) * LANES] = tb[r * n:(r + 1) * n]


def _dil_project(x, w, tables, units, *, batch, seq, tm=512):
    nseq = seq // tm
    out_shape, out_specs = [], []
    for jj in range(len(units)):
        d = DILATIONS[jj % N_GROUPS]
        out_shape.append(jax.ShapeDtypeStruct((batch, seq // d, d * D_MODEL), BF16))
        out_specs.append(pl.BlockSpec((None, tm // d, d * D_MODEL), lambda i, j: (i // nseq, i % nseq, 0)))
    x_specs = [pl.BlockSpec((tm, LANES), functools.partial(lambda i, j, c: (i, c), c=c))
               for c in range(N_LANE_BLOCKS)]
    tab_spec = pl.BlockSpec((tm, LANES), lambda i, j: (i % nseq, 0))
    return pl.pallas_call(
        functools.partial(_dil_proj_kernel, units=units, tm=tm),
        out_shape=out_shape,
        grid=(x.shape[0] // tm, len(units)),
        in_specs=x_specs + [pl.BlockSpec((D_MODEL, D_MODEL), lambda i, j: (0, j))] + [tab_spec] * (2 * N_GROUPS),
        out_specs=out_specs,
        scratch_shapes=[pltpu.VMEM((N_GROUPS, tm, D_MODEL), BF16)],
        compiler_params=_params("parallel", "arbitrary"),
        name="dil_proj",
    )(*([x] * N_LANE_BLOCKS), w, *tables)


def _dil_attn_kernel(q_ref, kc_ref, vc_ref, kp_ref, vp_ref, o_ref, lse_ref, *acc, dilation):
    blk = DIL_BLK
    d = dilation
    r = pl.program_id(2)
    no_prev = (pl.program_id(1) == 0).astype(jnp.int32)
    row = lax.broadcasted_iota(jnp.int32, (blk, blk), 0)
    col = lax.broadcasted_iota(jnp.int32, (blk, blk), 1)
    cur_ok = col <= row
    prev_ok = col >= row
    first_prev_ok = col >= row + no_prev * blk
    lane = lax.broadcasted_iota(jnp.int32, (blk, LANES), 1)
    nt = (((1,), (1,)), ((), ()))
    for c in range(q_ref.shape[0] // blk):
        rows = slice(c * blk, (c + 1) * blk)
        tok = rows if d == 1 else pl.ds(c * blk * d + r, blk, stride=d)
        lse_tile = jnp.zeros((blk, LANES), F32)
        for h in range(DIL_HEADS):
            cols = slice(h * DIL_HEAD_DIM, (h + 1) * DIL_HEAD_DIM)
            qh = q_ref[rows, cols]
            if c == 0:
                kp, vp, pmask = kp_ref[:, cols], vp_ref[:, cols], first_prev_ok
            else:
                prows = slice((c - 1) * blk, c * blk)
                kp, vp, pmask = kc_ref[prows, cols], vc_ref[prows, cols], prev_ok
            s_p = jnp.where(pmask, lax.dot_general(qh, kp, nt, preferred_element_type=F32), MASKED)
            s_c = jnp.where(cur_ok, lax.dot_general(qh, kc_ref[rows, cols], nt,
                                                    preferred_element_type=F32), MASKED)
            m = jnp.maximum(jnp.max(s_p, axis=-1, keepdims=True), jnp.max(s_c, axis=-1, keepdims=True))
            e_p = jnp.exp(s_p - m)
            e_c = jnp.exp(s_c - m)
            l = jnp.sum(e_p, axis=-1, keepdims=True) + jnp.sum(e_c, axis=-1, keepdims=True)
            o = jnp.dot(e_p.astype(BF16), vp, preferred_element_type=F32)
            o = (o + jnp.dot(e_c.astype(BF16), vc_ref[rows, cols], preferred_element_type=F32)) / l
            if d == 1:
                o_ref[rows, cols] = o.astype(BF16)
            else:
                acc[0][h, tok, :] = o
            lse_tile = jnp.where(lane == h, m + jnp.log(l), lse_tile)
        lse_ref[tok, :] = lse_tile

    if d > 1:
        @pl.when(r == d - 1)
        def _():
            for h in range(DIL_HEADS):
                o_ref[:, h * DIL_HEAD_DIM:(h + 1) * DIL_HEAD_DIM] = acc[0][h].astype(BF16)


def _dilated_attention(q, k, v, dilation, *, batch, seq, tq):
    d = dilation
    n_sub = seq // d
    nq = n_sub // tq
    per = tq // DIL_BLK
    rows_out = tq * d
    cur = pl.BlockSpec((None, tq, D_MODEL), lambda b, i, r: (b, i, r))
    prev = pl.BlockSpec((None, DIL_BLK, D_MODEL), lambda b, i, r: (b, jnp.maximum(i * per - 1, 0), r))
    scratch = [] if d == 1 else [pltpu.VMEM((DIL_HEADS, rows_out, DIL_HEAD_DIM), F32)]
    return pl.pallas_call(
        functools.partial(_dil_attn_kernel, dilation=d),
        out_shape=(jax.ShapeDtypeStruct((batch * seq, D_MODEL), BF16),
                   jax.ShapeDtypeStruct((batch * seq, LANES), F32)),
        grid=(batch, nq, d),
        in_specs=[cur, cur, cur, prev, prev],
        out_specs=(pl.BlockSpec((rows_out, D_MODEL), lambda b, i, r: (b * nq + i, 0)),
                   pl.BlockSpec((rows_out, LANES), lambda b, i, r: (b * nq + i, 0))),
        scratch_shapes=scratch,
        compiler_params=_params("parallel", "arbitrary", "arbitrary"),
        name=f"dil_attn_d{d}",
    )(q, k, v, k, v)


def _combine_out_ln_kernel(o1_ref, o2_ref, o3_ref, l1_ref, l2_ref, l3_ref, w_ref, x_ref, g_ref, b_ref,
                           out_ref, comb_ref):
    l1, l2, l3 = l1_ref[...], l2_ref[...], l3_ref[...]
    m = jnp.maximum(jnp.maximum(l1, l2), l3)
    e1, e2, e3 = jnp.exp(l1 - m), jnp.exp(l2 - m), jnp.exp(l3 - m)
    inv = 1.0 / (e1 + e2 + e3)
    w1, w2, w3 = e1 * inv, e2 * inv, e3 * inv
    for h in range(DIL_HEADS):
        cols = slice(h * DIL_HEAD_DIM, (h + 1) * DIL_HEAD_DIM)
        comb = (w1[:, h:h + 1] * o1_ref[:, cols].astype(F32)
                + w2[:, h:h + 1] * o2_ref[:, cols].astype(F32)
                + w3[:, h:h + 1] * o3_ref[:, cols].astype(F32))
        comb_ref[:, cols] = comb.astype(BF16)
    y = ALPHA * x_ref[...] + jnp.dot(comb_ref[...], w_ref[...], preferred_element_type=F32)
    out_ref[...] = _layernorm(y, g_ref[...], b_ref[...])


def _combine_out_ln(os, ls, w, x, g, b, *, tm=512):
    m = x.shape[0]
    tile = lambda width: pl.BlockSpec((tm, width), lambda i: (i, 0))
    return pl.pallas_call(
        _combine_out_ln_kernel,
        out_shape=jax.ShapeDtypeStruct((m, D_MODEL), F32),
        grid=(m // tm,),
        in_specs=[tile(D_MODEL)] * 3 + [tile(LANES)] * 3 + [
            _resident((D_MODEL, D_MODEL)), tile(D_MODEL), _resident((1, D_MODEL)), _resident((1, D_MODEL))],
        out_specs=tile(D_MODEL),
        scratch_shapes=[pltpu.VMEM((tm, D_MODEL), BF16)],
        compiler_params=_params("parallel"),
        name="combine_out_ln",
    )(*os, *ls, w, x, g, b)


def _retention_tables(seq):
    half = RET_QK_DIM // 2
    angle = 1.0 / (RET_THETA ** jnp.linspace(0.0, 1.0, half, dtype=F32))
    ang = jnp.arange(seq, dtype=F32)[:, None] * angle[None]
    c_sz = RET_CHUNK
    log_g = jnp.log(1.0 - 2.0 ** (-5.0 - jnp.arange(RET_HEADS, dtype=F32)))
    idx = jnp.arange(c_sz, dtype=F32)
    diff = idx[:, None] - idx[None, :]
    decay = jnp.where(diff[None] >= 0, jnp.exp(jnp.maximum(diff, 0.0)[None] * log_g[:, None, None]), 0.0)
    xi = jnp.exp((idx[None] + 1.0) * log_g[:, None])
    zeta = jnp.exp((c_sz - 1.0 - idx[None]) * log_g[:, None])
    g_chunk = jnp.exp(c_sz * log_g)
    xi_b = jnp.broadcast_to(xi[:, :, None], (RET_HEADS, c_sz, RET_V_DIM))
    zeta_b = jnp.broadcast_to(zeta[:, :, None], (RET_HEADS, c_sz, RET_QK_DIM))
    gc_b = jnp.broadcast_to(g_chunk[:, None, None], (RET_HEADS, 1, RET_V_DIM))
    return jnp.cos(ang), jnp.sin(ang), decay, xi_b, zeta_b, gc_b


def _rope_tables(seq, tm):
    half = ROT_DIMS // 2
    inv_freq = ROPE_THETA ** (-jnp.arange(0, ROT_DIMS, 2, dtype=F32) / ROT_DIMS)
    ang = jnp.arange(seq, dtype=F32)[:, None] * inv_freq[None]
    cos, sin = jnp.cos(ang), jnp.sin(ang)
    gap = LANES // 2 - half
    ones = jnp.ones((seq, gap), F32)
    zeros = jnp.zeros((seq, gap), F32)
    c_tab = jnp.concatenate([cos, ones, cos, ones], axis=1)
    s_tab = jnp.concatenate([-sin, zeros, sin, zeros], axis=1)
    tabs = []
    for d in DILATIONS:
        for t in (c_tab, s_tab):
            tabs.append(t.reshape(seq // tm, tm // d, d, LANES).transpose(0, 2, 1, 3).reshape(seq, LANES))
    return tabs


def _pair_split_cols(w, dim):
    k, n = w.shape
    return w.reshape(k, n // dim, dim // 2, 2).swapaxes(2, 3).reshape(k, n)


def _rope_cols(w):
    k, n = w.shape
    half = ROT_DIMS // 2
    gap = LANES // 2 - half
    wh = w.reshape(k, n // DIL_HEAD_DIM, DIL_HEAD_DIM)
    wh = jnp.concatenate([wh[:, :, :half], wh[:, :, ROT_DIMS:ROT_DIMS + gap],
                          wh[:, :, half:ROT_DIMS], wh[:, :, ROT_DIMS + gap:]], axis=2)
    return wh.reshape(k, n)


def kernel(x, ret_w_in, ret_w_out, kv_w, dil_w_q, dil_w_out, mlp_w_up, mlp_w_down, ln_g, ln_b):
    batch, seq, _ = x.shape
    xf = x.reshape(batch * seq, D_MODEL)

    cos_r, sin_r, decay, xi_b, zeta_b, gc_b = _retention_tables(seq)
    dil_tm = 512
    rope_tabs = _rope_tables(seq, dil_tm)
    attn_tq = (512, 512, 256)

    n_qk = 2 * D_MODEL
    n_k = N_GROUPS * D_MODEL
    q_scale = DIL_HEAD_DIM ** -0.5
    k_units = ((True, 1.0),) * N_GROUPS
    v_units = ((False, 1.0),) * N_GROUPS
    q_units = ((True, q_scale),) * N_GROUPS

    ks = vs = qs = None
    for l in range(DEPTH):
        g0, b0 = ln_g[l, 0][None], ln_b[l, 0][None]
        g1, b1 = ln_g[l, 1][None], ln_b[l, 1][None]
        if l < N_A:
            w_in = ret_w_in[l]
            w_in = jnp.concatenate([_pair_split_cols(w_in[:, :n_qk], RET_QK_DIM), w_in[:, n_qk:]], axis=1).astype(BF16)
            proj = _ret_project(xf, w_in, cos_r, sin_r, seq=seq)
            mix = _retention(proj, decay, xi_b, zeta_b, gc_b, batch=batch, seq=seq)
            xf = _out_ln(mix, ret_w_out[l].astype(BF16), xf, g0, b0)
        else:
            if l > N_A:
                w_q = _rope_cols(dil_w_q[l - N_A]).astype(BF16)
                qs = _dil_project(xf, w_q, rope_tabs, q_units, batch=batch, seq=seq, tm=dil_tm)
            outs, lses = [], []
            for gi, d in enumerate(DILATIONS):
                o, lse = _dilated_attention(qs[gi], ks[gi], vs[gi], d, batch=batch, seq=seq, tq=attn_tq[gi])
                outs.append(o)
                lses.append(lse)
            xf = _combine_out_ln(outs, lses, dil_w_out[l - N_A].astype(BF16), xf, g0, b0)
        xf = _mlp_ln(xf, mlp_w_up[l].astype(BF16), mlp_w_down[l].astype(BF16), g1, b1)
        if l == N_A - 1:
            w_kvq = jnp.concatenate([_rope_cols(kv_w[:, :n_k]), kv_w[:, n_k:],
                                     _rope_cols(dil_w_q[0])], axis=1).astype(BF16)
            kvq = _dil_project(xf, w_kvq, rope_tabs, k_units + v_units + q_units,
                               batch=batch, seq=seq, tm=dil_tm)
            ks, vs, qs = kvq[:N_GROUPS], kvq[N_GROUPS:2 * N_GROUPS], kvq[2 * N_GROUPS:]
    return xf.reshape(batch, seq, D_MODEL)
```

```python
import functools

import jax
import jax.numpy as jnp
from jax import lax
from jax.experimental import pallas as pl
from jax.experimental.pallas import tpu as pltpu

D_MODEL = 1024
DEPTH = 4
N_A = DEPTH // 2

RET_HEADS = 4
RET_QK_DIM = D_MODEL // RET_HEADS
RET_V_DIM = 2 * RET_QK_DIM
RET_CHUNK = 128
RET_THETA = 10000.0
RET_VW = RET_HEADS * RET_V_DIM

DIL_GROUPS = ((128, 1), (512, 4), (2048, 16))
DILATIONS = tuple(d for _, d in DIL_GROUPS)
N_GROUPS = len(DIL_GROUPS)
DIL_HEADS = 8
DIL_HEAD_DIM = D_MODEL // DIL_HEADS
ROT_DIMS = DIL_HEAD_DIM // 4
ROPE_THETA = 500000.0
DIL_BLK = 128

D_FF = 4 * D_MODEL
ALPHA = (2.0 * DEPTH) ** 0.25
LN_EPS = 1e-5
GN_EPS = 1e-6

LANES = 128
N_LANE_BLOCKS = D_MODEL // LANES
VMEM_LIMIT = 56 * 1024 * 1024
MASKED = -1e30

BF16 = jnp.bfloat16
F32 = jnp.float32


def _params(*sem):
    return pltpu.CompilerParams(dimension_semantics=sem, vmem_limit_bytes=VMEM_LIMIT)


def _resident(shape):
    return pl.BlockSpec(shape, lambda *_: (0,) * len(shape))


def _layernorm(y, g, b):
    mu = jnp.mean(y, axis=-1, keepdims=True)
    yc = y - mu
    var = jnp.mean(yc * yc, axis=-1, keepdims=True)
    return yc * lax.rsqrt(var + LN_EPS) * g + b


def _ret_proj_kernel(x_ref, w_ref, c_ref, s_ref, o_ref, xb_ref, *, n_rot, k_from, k_scale):
    j = pl.program_id(1)

    @pl.when(j == 0)
    def _():
        xb_ref[...] = x_ref[...].astype(BF16)

    @pl.when(j < n_rot)
    def _():
        acc = jnp.dot(xb_ref[...], w_ref[...], preferred_element_type=F32)
        sc = jnp.where(j >= k_from, k_scale, 1.0).astype(F32)
        c = c_ref[...] * sc
        s = s_ref[...] * sc
        for h in range(acc.shape[1] // (2 * LANES)):
            ev = slice(2 * h * LANES, (2 * h + 1) * LANES)
            od = slice((2 * h + 1) * LANES, (2 * h + 2) * LANES)
            e, o = acc[:, ev], acc[:, od]
            o_ref[:, ev] = (e * c - o * s).astype(BF16)
            o_ref[:, od] = (o * c + e * s).astype(BF16)

    @pl.when(j >= n_rot)
    def _():
        o_ref[...] = jnp.dot(xb_ref[...], w_ref[...], preferred_element_type=F32).astype(BF16)


def _ret_project(x, w, cos, sin, *, seq, tm=1024, tn=1024):
    m, kdim = x.shape
    n = w.shape[1]
    nseq = seq // tm
    kern = functools.partial(_ret_proj_kernel, n_rot=2 * D_MODEL // tn, k_from=D_MODEL // tn,
                             k_scale=RET_QK_DIM ** -0.5)
    return pl.pallas_call(
        kern,
        out_shape=jax.ShapeDtypeStruct((m, n), BF16),
        grid=(m // tm, n // tn),
        in_specs=[
            pl.BlockSpec((tm, kdim), lambda i, j: (i, 0)),
            pl.BlockSpec((kdim, tn), lambda i, j: (0, j)),
            pl.BlockSpec((tm, LANES), lambda i, j: (i % nseq, 0)),
            pl.BlockSpec((tm, LANES), lambda i, j: (i % nseq, 0)),
        ],
        out_specs=pl.BlockSpec((tm, tn), lambda i, j: (i, j)),
        scratch_shapes=[pltpu.VMEM((tm, kdim), BF16)],
        compiler_params=_params("parallel", "arbitrary"),
        name="ret_proj",
    )(x, w, cos, sin)


def _ret_kernel(q_ref, k_ref, v_ref, g_ref, dec_ref, xi_ref, zeta_ref, gc_ref, o_ref,
                r_ref, s_scr, kv_scr, rb_scr, o_scr):
    @pl.when(pl.program_id(2) == 0)
    def _():
        r_ref[...] = jnp.zeros_like(r_ref)

    c_sz = RET_CHUNK
    n_chunks = q_ref.shape[0] // c_sz
    for c in range(n_chunks):
        rows = slice(c * c_sz, (c + 1) * c_sz)
        s = lax.dot_general(q_ref[rows, :], k_ref[rows, :], (((1,), (1,)), ((), ())),
                            preferred_element_type=F32)
        s_scr[c] = (s * dec_ref[...]).astype(BF16)
        kz = (k_ref[rows, :].astype(F32) * zeta_ref[...]).astype(BF16)
        kv_scr[c] = lax.dot_general(kz, v_ref[rows, :], (((0,), (0,)), ((), ())),
                                    preferred_element_type=F32)
    r = r_ref[...]
    for c in range(n_chunks):
        rb_scr[c] = r.astype(BF16)
        r = gc_ref[...] * r + kv_scr[c]
    r_ref[...] = r
    for c in range(n_chunks):
        rows = slice(c * c_sz, (c + 1) * c_sz)
        o = jnp.dot(s_scr[c], v_ref[rows, :], preferred_element_type=F32)
        o_scr[rows, :] = o + jnp.dot(q_ref[rows, :], rb_scr[c], preferred_element_type=F32) * xi_ref[...]
    o = o_scr[...]
    of = o * lax.rsqrt(jnp.mean(o * o, axis=-1, keepdims=True) + GN_EPS)
    gate = g_ref[...].astype(F32)
    o_ref[...] = (gate * jax.nn.sigmoid(gate) * of).astype(BF16)


def _retention(proj, dec, xi, zeta, gc, *, batch, seq, tc=1024):
    ns = seq // tc
    n_chunks = tc // RET_CHUNK
    qb = RET_QK_DIM
    vb = RET_V_DIM
    row = lambda b, i: b * ns + i
    return pl.pallas_call(
        _ret_kernel,
        out_shape=jax.ShapeDtypeStruct((batch * seq, RET_VW), BF16),
        grid=(batch, RET_HEADS, ns),
        in_specs=[
            pl.BlockSpec((tc, qb), lambda b, h, i: (row(b, i), h)),
            pl.BlockSpec((tc, qb), lambda b, h, i: (row(b, i), D_MODEL // qb + h)),
            pl.BlockSpec((tc, vb), lambda b, h, i: (row(b, i), 2 * D_MODEL // vb + h)),
            pl.BlockSpec((tc, vb), lambda b, h, i: (row(b, i), (2 * D_MODEL + RET_VW) // vb + h)),
            pl.BlockSpec((None, RET_CHUNK, RET_CHUNK), lambda b, h, i: (h, 0, 0)),
            pl.BlockSpec((None, RET_CHUNK, vb), lambda b, h, i: (h, 0, 0)),
            pl.BlockSpec((None, RET_CHUNK, qb), lambda b, h, i: (h, 0, 0)),
            pl.BlockSpec((None, 1, vb), lambda b, h, i: (h, 0, 0)),
        ],
        out_specs=pl.BlockSpec((tc, vb), lambda b, h, i: (row(b, i), h)),
        scratch_shapes=[pltpu.VMEM((qb, vb), F32),
                        pltpu.VMEM((n_chunks, RET_CHUNK, RET_CHUNK), BF16),
                        pltpu.VMEM((n_chunks, qb, vb), F32),
                        pltpu.VMEM((n_chunks, qb, vb), BF16),
                        pltpu.VMEM((tc, vb), F32)],
        compiler_params=_params("parallel", "parallel", "arbitrary"),
        name="retention",
    )(proj, proj, proj, proj, dec, xi, zeta, gc)


def _out_ln_kernel(a_ref, w_ref, x_ref, g_ref, b_ref, o_ref):
    y = ALPHA * x_ref[...] + jnp.dot(a_ref[...], w_ref[...], preferred_element_type=F32)
    o_ref[...] = _layernorm(y, g_ref[...], b_ref[...])


def _out_ln(a, w, x, g, b, *, tm=512):
    m, kdim = a.shape
    return pl.pallas_call(
        _out_ln_kernel,
        out_shape=jax.ShapeDtypeStruct((m, D_MODEL), F32),
        grid=(m // tm,),
        in_specs=[
            pl.BlockSpec((tm, kdim), lambda i: (i, 0)),
            _resident((kdim, D_MODEL)),
            pl.BlockSpec((tm, D_MODEL), lambda i: (i, 0)),
            _resident((1, D_MODEL)),
            _resident((1, D_MODEL)),
        ],
        out_specs=pl.BlockSpec((tm, D_MODEL), lambda i: (i, 0)),
        compiler_params=_params("parallel"),
        name="out_ln",
    )(a, w, x, g, b)


def _mlp_ln_kernel(x_ref, wu_ref, wd_ref, g_ref, b_ref, o_ref, *, ff_chunk):
    x = x_ref[...]
    xb = x.astype(BF16)
    y = ALPHA * x
    for c in range(wu_ref.shape[1] // ff_chunk):
        cols = slice(c * ff_chunk, (c + 1) * ff_chunk)
        h = jnp.maximum(jnp.dot(xb, wu_ref[:, cols], preferred_element_type=F32), 0.0)
        y = y + jnp.dot((h * h).astype(BF16), wd_ref[cols, :], preferred_element_type=F32)
    o_ref[...] = _layernorm(y, g_ref[...], b_ref[...])


def _mlp_ln(x, wu, wd, g, b, *, tm=512, ff_chunk=1024):
    m = x.shape[0]
    return pl.pallas_call(
        functools.partial(_mlp_ln_kernel, ff_chunk=ff_chunk),
        out_shape=jax.ShapeDtypeStruct((m, D_MODEL), F32),
        grid=(m // tm,),
        in_specs=[
            pl.BlockSpec((tm, D_MODEL), lambda i: (i, 0)),
            _resident((D_MODEL, D_FF)),
            _resident((D_FF, D_MODEL)),
            _resident((1, D_MODEL)),
            _resident((1, D_MODEL)),
        ],
        out_specs=pl.BlockSpec((tm, D_MODEL), lambda i: (i, 0)),
        compiler_params=_params("parallel"),
        name="mlp_ln",
    )(x, wu, wd, g, b)


def _dil_proj_kernel(*refs, units, tm):
    nb = N_LANE_BLOCKS
    x_refs = refs[:nb]
    w_ref = refs[nb]
    tabs = refs[nb + 1:nb + 1 + 2 * N_GROUPS]
    outs = refs[nb + 1 + 2 * N_GROUPS:-1]
    xb_ref = refs[-1]
    j = pl.program_id(1)

    @pl.when(j == 0)
    def _():
        for gi, d in enumerate(DILATIONS):
            n = tm // d
            for c in range(nb):
                cols = slice(c * LANES, (c + 1) * LANES)
                for r in range(d):
                    piece = x_refs[c][...] if d == 1 else x_refs[c][pl.ds(r, n, stride=d), :]
                    xb_ref[gi, r * n:(r + 1) * n, cols] = piece.astype(BF16)

    for jj, (rope, scale) in enumerate(units):
        gi = jj % N_GROUPS
        d = DILATIONS[gi]
        n = tm // d

        @pl.when(j == jj)
        def _():
            acc = jnp.dot(xb_ref[gi], w_ref[...], preferred_element_type=F32)
            for h in range(DIL_HEADS):
                t = acc[:, h * LANES:(h + 1) * LANES]
                if rope:
                    t = t * tabs[2 * gi][...] + pltpu.roll(t, LANES // 2, axis=1) * tabs[2 * gi + 1][...]
                if scale != 1.0:
                    t = t * scale
                tb = t.astype(BF16)
                for r in range(d):
                    outs[jj][:, r * D_MODEL + h * LANES:r * D_MODEL + (h + 1) * LANES] = tb[r * n:(r + 1) * n]


def _dil_project(x, w, tables, units, *, batch, seq, tm=512):
    nseq = seq // tm
    out_shape, out_specs = [], []
    for jj in range(len(units)):
        d = DILATIONS[jj % N_GROUPS]
        out_shape.append(jax.ShapeDtypeStruct((batch, seq // d, d * D_MODEL), BF16))
        out_specs.append(pl.BlockSpec((None, tm // d, d * D_MODEL), lambda i, j: (i // nseq, i % nseq, 0)))
    x_specs = [pl.BlockSpec((tm, LANES), functools.partial(lambda i, j, c: (i, c), c=c))
               for c in range(N_LANE_BLOCKS)]
    tab_spec = pl.BlockSpec((tm, LANES), lambda i, j: (i % nseq, 0))
    return pl.pallas_call(
        functools.partial(_dil_proj_kernel, units=units, tm=tm),
        out_shape=out_shape,
        grid=(x.shape[0] // tm, len(units)),
        in_specs=x_specs + [pl.BlockSpec((D_MODEL, D_MODEL), lambda i, j: (0, j))] + [tab_spec] * (2 * N_GROUPS),
        out_specs=out_specs,
        scratch_shapes=[pltpu.VMEM((N_GROUPS, tm, D_MODEL), BF16)],
        compiler_params=_params("parallel", "arbitrary"),
        name="dil_proj",
    )(*([x] * N_LANE_BLOCKS), w, *tables)


def _dil_attn_kernel(q_ref, kc_ref, vc_ref, kp_ref, vp_ref, o_ref, lse_ref, s_scr, p_scr, *acc, dilation):
    blk = DIL_BLK
    d = dilation
    r = pl.program_id(2)
    no_prev = (pl.program_id(1) == 0).astype(jnp.int32)
    row = lax.broadcasted_iota(jnp.int32, (blk, blk), 0)
    col = lax.broadcasted_iota(jnp.int32, (blk, blk), 1)
    cur_ok = col <= row
    first_prev_ok = col >= row + no_prev * blk
    row2 = lax.broadcasted_iota(jnp.int32, (blk, 2 * blk), 0)
    col2 = lax.broadcasted_iota(jnp.int32, (blk, 2 * blk), 1)
    band_ok = jnp.logical_and(col2 >= row2, col2 <= row2 + blk)
    lane = lax.broadcasted_iota(jnp.int32, (blk, LANES), 1)
    nt = (((1,), (1,)), ((), ()))
    for c in range(q_ref.shape[0] // blk):
        rows = slice(c * blk, (c + 1) * blk)
        win = slice((c - 1) * blk, (c + 1) * blk)
        tok = rows if d == 1 else pl.ds(c * blk * d + r, blk, stride=d)
        for h in range(DIL_HEADS):
            cols = slice(h * DIL_HEAD_DIM, (h + 1) * DIL_HEAD_DIM)
            qh = q_ref[rows, cols]
            if c == 0:
                s_p = lax.dot_general(qh, kp_ref[:, cols], nt, preferred_element_type=F32)
                s_c = lax.dot_general(qh, kc_ref[rows, cols], nt, preferred_element_type=F32)
                s_scr[c, h, :, :blk] = jnp.where(first_prev_ok, s_p, MASKED)
                s_scr[c, h, :, blk:] = jnp.where(cur_ok, s_c, MASKED)
            else:
                s = lax.dot_general(qh, kc_ref[win, cols], nt, preferred_element_type=F32)
                s_scr[c, h] = jnp.where(band_ok, s, MASKED)
        s_all = s_scr[c]
        m = jnp.max(s_all, axis=-1, keepdims=True)
        e = jnp.exp(s_all - m)
        l = jnp.sum(e, axis=-1, keepdims=True)
        p_scr[c] = e.astype(BF16)
        inv = 1.0 / l
        lse = m + jnp.log(l)
        lse_tile = jnp.zeros((blk, LANES), F32)
        for h in range(DIL_HEADS):
            cols = slice(h * DIL_HEAD_DIM, (h + 1) * DIL_HEAD_DIM)
            if c == 0:
                o = jnp.dot(p_scr[c, h, :, :blk], vp_ref[:, cols], preferred_element_type=F32)
                o = o + jnp.dot(p_scr[c, h, :, blk:], vc_ref[rows, cols], preferred_element_type=F32)
            else:
                o = jnp.dot(p_scr[c, h], vc_ref[win, cols], preferred_element_type=F32)
            o = o * inv[h]
            if d == 1:
                o_ref[rows, cols] = o.astype(BF16)
            else:
                acc[0][h, tok, :] = o
            lse_tile = jnp.where(lane == h, lse[h], lse_tile)
        lse_ref[tok, :] = lse_tile

    if d > 1:
        @pl.when(r == d - 1)
        def _():
            for h in range(DIL_HEADS):
                o_ref[:, h * DIL_HEAD_DIM:(h + 1) * DIL_HEAD_DIM] = acc[0][h].astype(BF16)


def _dilated_attention(q, k, v, dilation, *, batch, seq, tq):
    d = dilation
    n_sub = seq // d
    nq = n_sub // tq
    per = tq // DIL_BLK
    rows_out = tq * d
    cur = pl.BlockSpec((None, tq, D_MODEL), lambda b, i, r: (b, i, r))
    prev = pl.BlockSpec((None, DIL_BLK, D_MODEL), lambda b, i, r: (b, jnp.maximum(i * per - 1, 0), r))
    scratch = [pltpu.VMEM((per, DIL_HEADS, DIL_BLK, 2 * DIL_BLK), F32),
               pltpu.VMEM((per, DIL_HEADS, DIL_BLK, 2 * DIL_BLK), BF16)]
    if d > 1:
        scratch.append(pltpu.VMEM((DIL_HEADS, rows_out, DIL_HEAD_DIM), F32))
    return pl.pallas_call(
        functools.partial(_dil_attn_kernel, dilation=d),
        out_shape=(jax.ShapeDtypeStruct((batch * seq, D_MODEL), BF16),
                   jax.ShapeDtypeStruct((batch * seq, LANES), F32)),
        grid=(batch, nq, d),
        in_specs=[cur, cur, cur, prev, prev],
        out_specs=(pl.BlockSpec((rows_out, D_MODEL), lambda b, i, r: (b * nq + i, 0)),
                   pl.BlockSpec((rows_out, LANES), lambda b, i, r: (b * nq + i, 0))),
        scratch_shapes=scratch,
        compiler_params=_params("parallel", "arbitrary", "arbitrary"),
        name=f"dil_attn_d{d}",
    )(q, k, v, k, v)


def _combine_out_ln_kernel(o1_ref, o2_ref, o3_ref, l1_ref, l2_ref, l3_ref, w_ref, x_ref, g_ref, b_ref,
                           out_ref, comb_ref):
    l1, l2, l3 = l1_ref[...], l2_ref[...], l3_ref[...]
    m = jnp.maximum(jnp.maximum(l1, l2), l3)
    e1, e2, e3 = jnp.exp(l1 - m), jnp.exp(l2 - m), jnp.exp(l3 - m)
    inv = 1.0 / (e1 + e2 + e3)
    w1, w2, w3 = e1 * inv, e2 * inv, e3 * inv
    for h in range(DIL_HEADS):
        cols = slice(h * DIL_HEAD_DIM, (h + 1) * DIL_HEAD_DIM)
        comb = (w1[:, h:h + 1] * o1_ref[:, cols].astype(F32)
                + w2[:, h:h + 1] * o2_ref[:, cols].astype(F32)
                + w3[:, h:h + 1] * o3_ref[:, cols].astype(F32))
        comb_ref[:, cols] = comb.astype(BF16)
    y = ALPHA * x_ref[...] + jnp.dot(comb_ref[...], w_ref[...], preferred_element_type=F32)
    out_ref[...] = _layernorm(y, g_ref[...], b_ref[...])


def _combine_out_ln(os, ls, w, x, g, b, *, tm=512):
    m = x.shape[0]
    tile = lambda width: pl.BlockSpec((tm, width), lambda i: (i, 0))
    return pl.pallas_call(
        _combine_out_ln_kernel,
        out_shape=jax.ShapeDtypeStruct((m, D_MODEL), F32),
        grid=(m // tm,),
        in_specs=[tile(D_MODEL)] * 3 + [tile(LANES)] * 3 + [
            _resident((D_MODEL, D_MODEL)), tile(D_MODEL), _resident((1, D_MODEL)), _resident((1, D_MODEL))],
        out_specs=tile(D_MODEL),
        scratch_shapes=[pltpu.VMEM((tm, D_MODEL), BF16)],
        compiler_params=_params("parallel"),
        name="combine_out_ln",
    )(*os, *ls, w, x, g, b)


def _retention_tables(seq):
    half = RET_QK_DIM // 2
    angle = 1.0 / (RET_THETA ** jnp.linspace(0.0, 1.0, half, dtype=F32))
    ang = jnp.arange(seq, dtype=F32)[:, None] * angle[None]
    c_sz = RET_CHUNK
    log_g = jnp.log(1.0 - 2.0 ** (-5.0 - jnp.arange(RET_HEADS, dtype=F32)))
    idx = jnp.arange(c_sz, dtype=F32)
    diff = idx[:, None] - idx[None, :]
    decay = jnp.where(diff[None] >= 0, jnp.exp(jnp.maximum(diff, 0.0)[None] * log_g[:, None, None]), 0.0)
    xi = jnp.exp((idx[None] + 1.0) * log_g[:, None])
    zeta = jnp.exp((c_sz - 1.0 - idx[None]) * log_g[:, None])
    g_chunk = jnp.exp(c_sz * log_g)
    xi_b = jnp.broadcast_to(xi[:, :, None], (RET_HEADS, c_sz, RET_V_DIM))
    zeta_b = jnp.broadcast_to(zeta[:, :, None], (RET_HEADS, c_sz, RET_QK_DIM))
    gc_b = jnp.broadcast_to(g_chunk[:, None, None], (RET_HEADS, 1, RET_V_DIM))
    return jnp.cos(ang), jnp.sin(ang), decay, xi_b, zeta_b, gc_b


def _rope_tables(seq, tm):
    half = ROT_DIMS // 2
    inv_freq = ROPE_THETA ** (-jnp.arange(0, ROT_DIMS, 2, dtype=F32) / ROT_DIMS)
    ang = jnp.arange(seq, dtype=F32)[:, None] * inv_freq[None]
    cos, sin = jnp.cos(ang), jnp.sin(ang)
    gap = LANES // 2 - half
    ones = jnp.ones((seq, gap), F32)
    zeros = jnp.zeros((seq, gap), F32)
    c_tab = jnp.concatenate([cos, ones, cos, ones], axis=1)
    s_tab = jnp.concatenate([-sin, zeros, sin, zeros], axis=1)
    tabs = []
    for d in DILATIONS:
        for t in (c_tab, s_tab):
            tabs.append(t.reshape(seq // tm, tm // d, d, LANES).transpose(0, 2, 1, 3).reshape(seq, LANES))
    return tabs


def _pair_split_cols(w, dim):
    k, n = w.shape
    return w.reshape(k, n // dim, dim // 2, 2).swapaxes(2, 3).reshape(k, n)


def _rope_cols(w):
    k, n = w.shape
    half = ROT_DIMS // 2
    gap = LANES // 2 - half
    wh = w.reshape(k, n // DIL_HEAD_DIM, DIL_HEAD_DIM)
    wh = jnp.concatenate([wh[:, :, :half], wh[:, :, ROT_DIMS:ROT_DIMS + gap],
                          wh[:, :, half:ROT_DIMS], wh[:, :, ROT_DIMS + gap:]], axis=2)
    return wh.reshape(k, n)


def kernel(x, ret_w_in, ret_w_out, kv_w, dil_w_q, dil_w_out, mlp_w_up, mlp_w_down, ln_g, ln_b):
    batch, seq, _ = x.shape
    xf = x.reshape(batch * seq, D_MODEL)

    cos_r, sin_r, decay, xi_b, zeta_b, gc_b = _retention_tables(seq)
    dil_tm = 512
    rope_tabs = _rope_tables(seq, dil_tm)
    attn_tq = (512, 512, 256)

    n_qk = 2 * D_MODEL
    n_k = N_GROUPS * D_MODEL
    q_scale = DIL_HEAD_DIM ** -0.5
    k_units = ((True, 1.0),) * N_GROUPS
    v_units = ((False, 1.0),) * N_GROUPS
    q_units = ((True, q_scale),) * N_GROUPS

    ks = vs = qs = None
    for l in range(DEPTH):
        g0, b0 = ln_g[l, 0][None], ln_b[l, 0][None]
        g1, b1 = ln_g[l, 1][None], ln_b[l, 1][None]
        if l < N_A:
            w_in = ret_w_in[l]
            w_in = jnp.concatenate([_pair_split_cols(w_in[:, :n_qk], RET_QK_DIM), w_in[:, n_qk:]], axis=1).astype(BF16)
            proj = _ret_project(xf, w_in, cos_r, sin_r, seq=seq)
            mix = _retention(proj, decay, xi_b, zeta_b, gc_b, batch=batch, seq=seq)
            xf = _out_ln(mix, ret_w_out[l].astype(BF16), xf, g0, b0)
        else:
            if l > N_A:
                w_q = _rope_cols(dil_w_q[l - N_A]).astype(BF16)
                qs = _dil_project(xf, w_q, rope_tabs, q_units, batch=batch, seq=seq, tm=dil_tm)
            outs, lses = [], []
            for gi, d in enumerate(DILATIONS):
                o, lse = _dilated_attention(qs[gi], ks[gi], vs[gi], d, batch=batch, seq=seq, tq=attn_tq[gi])
                outs.append(o)
                lses.append(lse)
            xf = _combine_out_ln(outs, lses, dil_w_out[l - N_A].astype(BF16), xf, g0, b0)
        xf = _mlp_ln(xf, mlp_w_up[l].astype(BF16), mlp_w_down[l].astype(BF16), g1, b1)
        if l == N_A - 1:
            w_kvq = jnp.concatenate([_rope_cols(kv_w[:, :n_k]), kv_w[:, n_k:],
                                     _rope_cols(dil_w_q[0])], axis=1).astype(BF16)
            kvq = _dil_project(xf, w_kvq, rope_tabs, k_units + v_units + q_units,
                               batch=batch, seq=seq, tm=dil_tm)
            ks, vs, qs = kvq[:N_GROUPS], kvq[N_GROUPS:2 * N_GROUPS], kvq[2 * N_GROUPS:]
    return xf.reshape(batch, seq, D_MODEL)
```

```python
import functools

import jax
import jax.numpy as jnp
from jax import lax
from jax.experimental import pallas as pl
from jax.experimental.pallas import tpu as pltpu

D_MODEL = 1024
DEPTH = 4
N_A = DEPTH // 2

RET_HEADS = 4
RET_QK_DIM = D_MODEL // RET_HEADS
RET_V_DIM = 2 * RET_QK_DIM
RET_CHUNK = 128
RET_THETA = 10000.0
RET_VW = RET_HEADS * RET_V_DIM

DIL_GROUPS = ((128, 1), (512, 4), (2048, 16))
DILATIONS = tuple(d for _, d in DIL_GROUPS)
N_GROUPS = len(DIL_GROUPS)
DIL_HEADS = 8
DIL_HEAD_DIM = D_MODEL // DIL_HEADS
ROT_DIMS = DIL_HEAD_DIM // 4
ROPE_THETA = 500000.0
DIL_BLK = 128

D_FF = 4 * D_MODEL
ALPHA = (2.0 * DEPTH) ** 0.25
LN_EPS = 1e-5
GN_EPS = 1e-6

LANES = 128
N_LANE_BLOCKS = D_MODEL // LANES
VMEM_LIMIT = 56 * 1024 * 1024
MASKED = -1e30

BF16 = jnp.bfloat16
F32 = jnp.float32


def _params(*sem):
    return pltpu.CompilerParams(dimension_semantics=sem, vmem_limit_bytes=VMEM_LIMIT)


def _resident(shape):
    return pl.BlockSpec(shape, lambda *_: (0,) * len(shape), pipeline_mode=pl.Buffered(1))


def _resident_layer(shape, layer):
    return pl.BlockSpec((None,) + tuple(shape), lambda *_: (layer,) + (0,) * len(shape),
                        pipeline_mode=pl.Buffered(1))


def _layernorm(y, g, b):
    mu = jnp.mean(y, axis=-1, keepdims=True)
    yc = y - mu
    var = jnp.mean(yc * yc, axis=-1, keepdims=True)
    return yc * lax.rsqrt(var + LN_EPS) * g + b


def _ret_proj_kernel(x_ref, w_ref, c_ref, s_ref, o_ref, *, tn):
    xb = x_ref[...].astype(BF16)
    c_q, s_q = c_ref[...], s_ref[...]
    k_scale = RET_QK_DIM ** -0.5
    c_k, s_k = c_q * k_scale, s_q * k_scale
    for j in range(w_ref.shape[1] // tn):
        at = j * tn
        acc = jnp.dot(xb, w_ref[:, at:at + tn], preferred_element_type=F32)
        if at >= 2 * D_MODEL:
            o_ref[:, at:at + tn] = acc.astype(BF16)
            continue
        c, s = (c_q, s_q) if at < D_MODEL else (c_k, s_k)
        for h in range(tn // (2 * LANES)):
            ev = 2 * h * LANES
            od = ev + LANES
            e, o = acc[:, ev:ev + LANES], acc[:, od:od + LANES]
            o_ref[:, at + ev:at + ev + LANES] = (e * c - o * s).astype(BF16)
            o_ref[:, at + od:at + od + LANES] = (o * c + e * s).astype(BF16)


def _ret_project(x, w_stack, layer, cos, sin, *, seq, tm=512, tn=1024):
    m, kdim = x.shape
    n = w_stack.shape[2]
    nseq = seq // tm
    return pl.pallas_call(
        functools.partial(_ret_proj_kernel, tn=tn),
        out_shape=jax.ShapeDtypeStruct((m, n), BF16),
        grid=(m // tm,),
        in_specs=[
            pl.BlockSpec((tm, kdim), lambda i: (i, 0)),
            _resident_layer((kdim, n), layer),
            pl.BlockSpec((tm, LANES), lambda i: (i % nseq, 0)),
            pl.BlockSpec((tm, LANES), lambda i: (i % nseq, 0)),
        ],
        out_specs=pl.BlockSpec((tm, n), lambda i: (i, 0)),
        compiler_params=_params("parallel"),
        name="ret_proj",
    )(x, w_stack, cos, sin)


def _ret_kernel(q_ref, k_ref, v_ref, g_ref, dec_ref, xi_ref, zeta_ref, gc_ref, o_ref,
                r_ref, s_scr, kv_scr, rb_scr, o_scr):
    @pl.when(pl.program_id(2) == 0)
    def _():
        r_ref[...] = jnp.zeros_like(r_ref)

    c_sz = RET_CHUNK
    n_chunks = q_ref.shape[0] // c_sz
    for c in range(n_chunks):
        rows = slice(c * c_sz, (c + 1) * c_sz)
        s = lax.dot_general(q_ref[rows, :], k_ref[rows, :], (((1,), (1,)), ((), ())),
                            preferred_element_type=F32)
        s_scr[c] = (s * dec_ref[...]).astype(BF16)
        kz = (k_ref[rows, :].astype(F32) * zeta_ref[...]).astype(BF16)
        kv_scr[c] = lax.dot_general(kz, v_ref[rows, :], (((0,), (0,)), ((), ())),
                                    preferred_element_type=F32)
    r = r_ref[...]
    for c in range(n_chunks):
        rb_scr[c] = r.astype(BF16)
        r = gc_ref[...] * r + kv_scr[c]
    r_ref[...] = r
    for c in range(n_chunks):
        rows = slice(c * c_sz, (c + 1) * c_sz)
        o = jnp.dot(s_scr[c], v_ref[rows, :], preferred_element_type=F32)
        o_scr[rows, :] = o + jnp.dot(q_ref[rows, :], rb_scr[c], preferred_element_type=F32) * xi_ref[...]
    o = o_scr[...]
    of = o * lax.rsqrt(jnp.mean(o * o, axis=-1, keepdims=True) + GN_EPS)
    gate = g_ref[...].astype(F32)
    o_ref[...] = (gate * jax.nn.sigmoid(gate) * of).astype(BF16)


def _retention(proj, dec, xi, zeta, gc, *, batch, seq, tc=1024):
    ns = seq // tc
    n_chunks = tc // RET_CHUNK
    qb = RET_QK_DIM
    vb = RET_V_DIM
    row = lambda b, i: b * ns + i
    return pl.pallas_call(
        _ret_kernel,
        out_shape=jax.ShapeDtypeStruct((batch * seq, RET_VW), BF16),
        grid=(batch, RET_HEADS, ns),
        in_specs=[
            pl.BlockSpec((tc, qb), lambda b, h, i: (row(b, i), h)),
            pl.BlockSpec((tc, qb), lambda b, h, i: (row(b, i), D_MODEL // qb + h)),
            pl.BlockSpec((tc, vb), lambda b, h, i: (row(b, i), 2 * D_MODEL // vb + h)),
            pl.BlockSpec((tc, vb), lambda b, h, i: (row(b, i), (2 * D_MODEL + RET_VW) // vb + h)),
            pl.BlockSpec((None, RET_CHUNK, RET_CHUNK), lambda b, h, i: (h, 0, 0)),
            pl.BlockSpec((None, RET_CHUNK, vb), lambda b, h, i: (h, 0, 0)),
            pl.BlockSpec((None, RET_CHUNK, qb), lambda b, h, i: (h, 0, 0)),
            pl.BlockSpec((None, 1, vb), lambda b, h, i: (h, 0, 0)),
        ],
        out_specs=pl.BlockSpec((tc, vb), lambda b, h, i: (row(b, i), h)),
        scratch_shapes=[pltpu.VMEM((qb, vb), F32),
                        pltpu.VMEM((n_chunks, RET_CHUNK, RET_CHUNK), BF16),
                        pltpu.VMEM((n_chunks, qb, vb), F32),
                        pltpu.VMEM((n_chunks, qb, vb), BF16),
                        pltpu.VMEM((tc, vb), F32)],
        compiler_params=_params("parallel", "parallel", "arbitrary"),
        name="retention",
    )(proj, proj, proj, proj, dec, xi, zeta, gc)


def _finish_layer(mix, x, w_out_ref, ln_ref, wu_ref, wd_ref, o_ref, *, ff_chunk):
    x1 = _layernorm(ALPHA * x + jnp.dot(mix, w_out_ref[...], preferred_element_type=F32),
                    ln_ref[0:1, :], ln_ref[1:2, :])
    xb = x1.astype(BF16)
    y = ALPHA * x1
    for c in range(wu_ref.shape[1] // ff_chunk):
        cols = slice(c * ff_chunk, (c + 1) * ff_chunk)
        h = jnp.maximum(jnp.dot(xb, wu_ref[:, cols], preferred_element_type=F32), 0.0)
        y = y + jnp.dot((h * h).astype(BF16), wd_ref[cols, :], preferred_element_type=F32)
    o_ref[...] = _layernorm(y, ln_ref[2:3, :], ln_ref[3:4, :])


def _ret_tail_kernel(a_ref, x_ref, w_out_ref, ln_ref, wu_ref, wd_ref, o_ref, *, ff_chunk):
    _finish_layer(a_ref[...], x_ref[...], w_out_ref, ln_ref, wu_ref, wd_ref, o_ref, ff_chunk=ff_chunk)


def _dil_tail_kernel(o1_ref, o2_ref, o3_ref, l1_ref, l2_ref, l3_ref, x_ref, w_out_ref, ln_ref, wu_ref, wd_ref,
                     o_ref, comb_ref, *, ff_chunk):
    l1, l2, l3 = l1_ref[...], l2_ref[...], l3_ref[...]
    m = jnp.maximum(jnp.maximum(l1, l2), l3)
    e1, e2, e3 = jnp.exp(l1 - m), jnp.exp(l2 - m), jnp.exp(l3 - m)
    inv = 1.0 / (e1 + e2 + e3)
    w1, w2 = e1 * inv, e2 * inv
    for h in range(DIL_HEADS):
        cols = slice(h * DIL_HEAD_DIM, (h + 1) * DIL_HEAD_DIM)
        o3 = o3_ref[:, cols].astype(F32)
        comb = (o3 + w1[:, h:h + 1] * (o1_ref[:, cols].astype(F32) - o3)
                + w2[:, h:h + 1] * (o2_ref[:, cols].astype(F32) - o3))
        comb_ref[:, cols] = comb.astype(BF16)
    _finish_layer(comb_ref[...], x_ref[...], w_out_ref, ln_ref, wu_ref, wd_ref, o_ref, ff_chunk=ff_chunk)


def _layer_tail(mixes, lses, x, w_out_stack, mixer_layer, ln, wu_stack, wd_stack, layer, *, tm=512, ff_chunk=1024):
    m = x.shape[0]
    tile = lambda width: pl.BlockSpec((tm, width), lambda i: (i, 0))
    dilated = len(mixes) > 1
    kdim = w_out_stack.shape[1]
    in_specs = ([tile(a.shape[1]) for a in mixes] + [tile(LANES)] * len(lses) + [
        tile(D_MODEL), _resident_layer((kdim, D_MODEL), mixer_layer), _resident((4, D_MODEL)),
        _resident_layer((D_MODEL, D_FF), layer), _resident_layer((D_FF, D_MODEL), layer)])
    kern = _dil_tail_kernel if dilated else _ret_tail_kernel
    return pl.pallas_call(
        functools.partial(kern, ff_chunk=ff_chunk),
        out_shape=jax.ShapeDtypeStruct((m, D_MODEL), F32),
        grid=(m // tm,),
        in_specs=in_specs,
        out_specs=tile(D_MODEL),
        scratch_shapes=[pltpu.VMEM((tm, D_MODEL), BF16)] if dilated else [],
        compiler_params=_params("parallel"),
        name="dil_tail" if dilated else "ret_tail",
    )(*mixes, *lses, x, w_out_stack, ln, wu_stack, wd_stack)


def _dil_proj_kernel(*refs, units, tm):
    nb = N_LANE_BLOCKS
    x_refs = refs[:nb]
    w_ref, cos_ref, sin_ref = refs[nb:nb + 3]
    outs = refs[nb + 3:-2]
    xb_ref, tab_ref = refs[-2:]

    def by_residue(ref, d, r):
        return ref[...] if d == 1 else ref[pl.ds(r, tm // d, stride=d), :]

    for gi, d in enumerate(DILATIONS):
        n = tm // d
        for r in range(d):
            rows = slice(r * n, (r + 1) * n)
            for c in range(nb):
                xb_ref[gi, rows, c * LANES:(c + 1) * LANES] = by_residue(x_refs[c], d, r).astype(BF16)
            tab_ref[gi, 0, rows, :] = by_residue(cos_ref, d, r)
            tab_ref[gi, 1, rows, :] = by_residue(sin_ref, d, r)

    for jj, (rope, scale) in enumerate(units):
        gi = jj % N_GROUPS
        d = DILATIONS[gi]
        n = tm // d
        acc = jnp.dot(xb_ref[gi], w_ref[:, jj * D_MODEL:(jj + 1) * D_MODEL], preferred_element_type=F32)
        for h in range(DIL_HEADS):
            t = acc[:, h * LANES:(h + 1) * LANES]
            if rope:
                t = t * tab_ref[gi, 0] + pltpu.roll(t, LANES // 2, axis=1) * tab_ref[gi, 1]
            if scale != 1.0:
                t = t * scale
            tb = t.astype(BF16)
            for r in range(d):
                outs[jj][:, r * D_MODEL + h * LANES:r * D_MODEL + (h + 1) * LANES] = tb[r * n:(r + 1) * n]


def _dil_project(x, w, tables, units, *, batch, seq, tm=512):
    nseq = seq // tm
    out_shape, out_specs = [], []
    for jj in range(len(units)):
        d = DILATIONS[jj % N_GROUPS]
        out_shape.append(jax.ShapeDtypeStruct((batch, seq // d, d * D_MODEL), BF16))
        out_specs.append(pl.BlockSpec((None, tm // d, d * D_MODEL), lambda i: (i // nseq, i % nseq, 0)))
    x_specs = [pl.BlockSpec((tm, LANES), functools.partial(lambda i, c: (i, c), c=c))
               for c in range(N_LANE_BLOCKS)]
    tab_spec = pl.BlockSpec((tm, LANES), lambda i: (i % nseq, 0))
    return pl.pallas_call(
        functools.partial(_dil_proj_kernel, units=units, tm=tm),
        out_shape=out_shape,
        grid=(x.shape[0] // tm,),
        in_specs=x_specs + [_resident(w.shape), tab_spec, tab_spec],
        out_specs=out_specs,
        scratch_shapes=[pltpu.VMEM((N_GROUPS, tm, D_MODEL), BF16),
                        pltpu.VMEM((N_GROUPS, 2, tm, LANES), F32)],
        compiler_params=_params("parallel"),
        name="dil_proj",
    )(*([x] * N_LANE_BLOCKS), w, *tables)


def _dil_attn_kernel(q_ref, kc_ref, vc_ref, kp_ref, vp_ref, o_ref, lse_ref, s_scr, p_scr, *acc, dilation):
    blk = DIL_BLK
    d = dilation
    r = pl.program_id(2)
    no_prev = (pl.program_id(1) == 0).astype(jnp.int32)
    row = lax.broadcasted_iota(jnp.int32, (blk, blk), 0)
    col = lax.broadcasted_iota(jnp.int32, (blk, blk), 1)
    cur_ok = col <= row
    first_prev_ok = col >= row + no_prev * blk
    row2 = lax.broadcasted_iota(jnp.int32, (blk, 2 * blk), 0)
    col2 = lax.broadcasted_iota(jnp.int32, (blk, 2 * blk), 1)
    band_ok = jnp.logical_and(col2 >= row2, col2 <= row2 + blk)
    lane = lax.broadcasted_iota(jnp.int32, (blk, LANES), 1)
    nt = (((1,), (1,)), ((), ()))
    for c in range(q_ref.shape[0] // blk):
        rows = slice(c * blk, (c + 1) * blk)
        win = slice((c - 1) * blk, (c + 1) * blk)
        tok = rows if d == 1 else pl.ds(c * blk * d + r, blk, stride=d)
        for h in range(DIL_HEADS):
            cols = slice(h * DIL_HEAD_DIM, (h + 1) * DIL_HEAD_DIM)
            qh = q_ref[rows, cols]
            if c == 0:
                s_p = lax.dot_general(qh, kp_ref[:, cols], nt, preferred_element_type=F32)
                s_c = lax.dot_general(qh, kc_ref[rows, cols], nt, preferred_element_type=F32)
                s_scr[c, h, :, :blk] = jnp.where(first_prev_ok, s_p, MASKED)
                s_scr[c, h, :, blk:] = jnp.where(cur_ok, s_c, MASKED)
            else:
                s = lax.dot_general(qh, kc_ref[win, cols], nt, preferred_element_type=F32)
                s_scr[c, h] = jnp.where(band_ok, s, MASKED)
        s_all = s_scr[c]
        m = jnp.max(s_all, axis=-1, keepdims=True)
        e = jnp.exp(s_all - m)
        l = jnp.sum(e, axis=-1, keepdims=True)
        p_scr[c] = e.astype(BF16)
        inv = 1.0 / l
        lse = m + jnp.log(l)
        lse_tile = jnp.zeros((blk, LANES), F32)
        for h in range(DIL_HEADS):
            cols = slice(h * DIL_HEAD_DIM, (h + 1) * DIL_HEAD_DIM)
            if c == 0:
                o = jnp.dot(p_scr[c, h, :, :blk], vp_ref[:, cols], preferred_element_type=F32)
                o = o + jnp.dot(p_scr[c, h, :, blk:], vc_ref[rows, cols], preferred_element_type=F32)
            else:
                o = jnp.dot(p_scr[c, h], vc_ref[win, cols], preferred_element_type=F32)
            o = o * inv[h]
            if d == 1:
                o_ref[rows, cols] = o.astype(BF16)
            else:
                acc[0][h, tok, :] = o
            lse_tile = jnp.where(lane == h, lse[h], lse_tile)
        lse_ref[tok, :] = lse_tile

    if d > 1:
        @pl.when(r == d - 1)
        def _():
            for h in range(DIL_HEADS):
                o_ref[:, h * DIL_HEAD_DIM:(h + 1) * DIL_HEAD_DIM] = acc[0][h].astype(BF16)


def _dilated_attention(q, k, v, dilation, *, batch, seq, tq):
    d = dilation
    n_sub = seq // d
    nq = n_sub // tq
    per = tq // DIL_BLK
    rows_out = tq * d
    cur = pl.BlockSpec((None, tq, D_MODEL), lambda b, i, r: (b, i, r))
    prev = pl.BlockSpec((None, DIL_BLK, D_MODEL), lambda b, i, r: (b, jnp.maximum(i * per - 1, 0), r))
    scratch = [pltpu.VMEM((per, DIL_HEADS, DIL_BLK, 2 * DIL_BLK), F32),
               pltpu.VMEM((per, DIL_HEADS, DIL_BLK, 2 * DIL_BLK), BF16)]
    if d > 1:
        scratch.append(pltpu.VMEM((DIL_HEADS, rows_out, DIL_HEAD_DIM), F32))
    return pl.pallas_call(
        functools.partial(_dil_attn_kernel, dilation=d),
        out_shape=(jax.ShapeDtypeStruct((batch * seq, D_MODEL), BF16),
                   jax.ShapeDtypeStruct((batch * seq, LANES), F32)),
        grid=(batch, nq, d),
        in_specs=[cur, cur, cur, prev, prev],
        out_specs=(pl.BlockSpec((rows_out, D_MODEL), lambda b, i, r: (b * nq + i, 0)),
                   pl.BlockSpec((rows_out, LANES), lambda b, i, r: (b * nq + i, 0))),
        scratch_shapes=scratch,
        compiler_params=_params("parallel", "arbitrary", "arbitrary"),
        name=f"dil_attn_d{d}",
    )(q, k, v, k, v)


def _retention_tables(seq):
    half = RET_QK_DIM // 2
    angle = 1.0 / (RET_THETA ** jnp.linspace(0.0, 1.0, half, dtype=F32))
    ang = jnp.arange(seq, dtype=F32)[:, None] * angle[None]
    c_sz = RET_CHUNK
    log_g = jnp.log(1.0 - 2.0 ** (-5.0 - jnp.arange(RET_HEADS, dtype=F32)))
    idx = jnp.arange(c_sz, dtype=F32)
    diff = idx[:, None] - idx[None, :]
    decay = jnp.where(diff[None] >= 0, jnp.exp(jnp.maximum(diff, 0.0)[None] * log_g[:, None, None]), 0.0)
    xi = jnp.exp((idx[None] + 1.0) * log_g[:, None])
    zeta = jnp.exp((c_sz - 1.0 - idx[None]) * log_g[:, None])
    g_chunk = jnp.exp(c_sz * log_g)
    xi_b = jnp.broadcast_to(xi[:, :, None], (RET_HEADS, c_sz, RET_V_DIM))
    zeta_b = jnp.broadcast_to(zeta[:, :, None], (RET_HEADS, c_sz, RET_QK_DIM))
    gc_b = jnp.broadcast_to(g_chunk[:, None, None], (RET_HEADS, 1, RET_V_DIM))
    return jnp.cos(ang), jnp.sin(ang), decay, xi_b, zeta_b, gc_b


def _rope_tables(seq):
    half = ROT_DIMS // 2
    inv_freq = ROPE_THETA ** (-jnp.arange(0, ROT_DIMS, 2, dtype=F32) / ROT_DIMS)
    ang = jnp.arange(seq, dtype=F32)[:, None] * inv_freq[None]
    cos, sin = jnp.cos(ang), jnp.sin(ang)
    gap = LANES // 2 - half
    ones = jnp.ones((seq, gap), F32)
    zeros = jnp.zeros((seq, gap), F32)
    return (jnp.concatenate([cos, ones, cos, ones], axis=1),
            jnp.concatenate([-sin, zeros, sin, zeros], axis=1))


def _pair_split_cols(w, dim):
    lead, n = w.shape[:-1], w.shape[-1]
    return w.reshape(*lead, n // dim, dim // 2, 2).swapaxes(-1, -2).reshape(*lead, n)


def _rope_cols(w):
    k, n = w.shape
    half = ROT_DIMS // 2
    gap = LANES // 2 - half
    wh = w.reshape(k, n // DIL_HEAD_DIM, DIL_HEAD_DIM)
    wh = jnp.concatenate([wh[:, :, :half], wh[:, :, ROT_DIMS:ROT_DIMS + gap],
                          wh[:, :, half:ROT_DIMS], wh[:, :, ROT_DIMS + gap:]], axis=2)
    return wh.reshape(k, n)


def kernel(x, ret_w_in, ret_w_out, kv_w, dil_w_q, dil_w_out, mlp_w_up, mlp_w_down, ln_g, ln_b):
    batch, seq, _ = x.shape
    xf = x.reshape(batch * seq, D_MODEL)

    cos_r, sin_r, decay, xi_b, zeta_b, gc_b = _retention_tables(seq)
    dil_tm = 512
    rope_tabs = _rope_tables(seq)
    attn_tq = (512, 512, 256)

    n_qk = 2 * D_MODEL
    n_k = N_GROUPS * D_MODEL
    q_scale = DIL_HEAD_DIM ** -0.5
    k_units = ((True, 1.0),) * N_GROUPS
    v_units = ((False, 1.0),) * N_GROUPS
    q_units = ((True, q_scale),) * N_GROUPS

    w_in = jnp.concatenate([_pair_split_cols(ret_w_in[:, :, :n_qk], RET_QK_DIM), ret_w_in[:, :, n_qk:]],
                           axis=2).astype(BF16)
    ret_w_out_b = ret_w_out.astype(BF16)
    dil_w_out_b = dil_w_out.astype(BF16)
    w_up = mlp_w_up.astype(BF16)
    w_down = mlp_w_down.astype(BF16)
    ln = jnp.stack([ln_g[:, 0], ln_b[:, 0], ln_g[:, 1], ln_b[:, 1]], axis=1)

    ks = vs = qs = None
    for l in range(DEPTH):
        if l < N_A:
            proj = _ret_project(xf, w_in, l, cos_r, sin_r, seq=seq)
            mix = _retention(proj, decay, xi_b, zeta_b, gc_b, batch=batch, seq=seq)
            xf = _layer_tail([mix], [], xf, ret_w_out_b, l, ln[l], w_up, w_down, l)
        else:
            if l > N_A:
                w_q = _rope_cols(dil_w_q[l - N_A]).astype(BF16)
                qs = _dil_project(xf, w_q, rope_tabs, q_units, batch=batch, seq=seq, tm=dil_tm)
            outs, lses = [], []
            for gi, d in enumerate(DILATIONS):
                o, lse = _dilated_attention(qs[gi], ks[gi], vs[gi], d, batch=batch, seq=seq, tq=attn_tq[gi])
                outs.append(o)
                lses.append(lse)
            xf = _layer_tail(outs, lses, xf, dil_w_out_b, l - N_A, ln[l], w_up, w_down, l)
        if l == N_A - 1:
            w_kvq = jnp.concatenate([_rope_cols(kv_w[:, :n_k]), kv_w[:, n_k:],
                                     _rope_cols(dil_w_q[0])], axis=1).astype(BF16)
            kvq = _dil_project(xf, w_kvq, rope_tabs, k_units + v_units + q_units,
                               batch=batch, seq=seq, tm=dil_tm)
            ks, vs, qs = kvq[:N_GROUPS], kvq[N_GROUPS:2 * N_GROUPS], kvq[2 * N_GROUPS:]
    return xf.reshape(batch, seq, D_MODEL)
```

```python
import functools

import jax
import jax.numpy as jnp
from jax import lax
from jax.experimental import pallas as pl
from jax.experimental.pallas import tpu as pltpu

D_MODEL = 1024
DEPTH = 4
N_A = DEPTH // 2

RET_HEADS = 4
RET_QK_DIM = D_MODEL // RET_HEADS
RET_V_DIM = 2 * RET_QK_DIM
RET_CHUNK = 256
RET_THETA = 10000.0
RET_VW = RET_HEADS * RET_V_DIM

DIL_GROUPS = ((128, 1), (512, 4), (2048, 16))
DILATIONS = tuple(d for _, d in DIL_GROUPS)
N_GROUPS = len(DIL_GROUPS)
DIL_HEADS = 8
DIL_HEAD_DIM = D_MODEL // DIL_HEADS
ROT_DIMS = DIL_HEAD_DIM // 4
ROPE_THETA = 500000.0
DIL_BLK = 128

D_FF = 4 * D_MODEL
ALPHA = (2.0 * DEPTH) ** 0.25
LN_EPS = 1e-5
GN_EPS = 1e-6

LANES = 128
N_LANE_BLOCKS = D_MODEL // LANES
VMEM_LIMIT = 56 * 1024 * 1024
MASKED = -1e30

BF16 = jnp.bfloat16
F32 = jnp.float32


def _params(*sem):
    return pltpu.CompilerParams(dimension_semantics=sem, vmem_limit_bytes=VMEM_LIMIT)


def _resident(shape):
    return pl.BlockSpec(shape, lambda *_: (0,) * len(shape), pipeline_mode=pl.Buffered(1))


def _resident_layer(shape, layer):
    return pl.BlockSpec((None,) + tuple(shape), lambda *_: (layer,) + (0,) * len(shape),
                        pipeline_mode=pl.Buffered(1))


def _layernorm(y, g, b):
    mu = jnp.mean(y, axis=-1, keepdims=True)
    yc = y - mu
    var = jnp.mean(yc * yc, axis=-1, keepdims=True)
    return yc * lax.rsqrt(var + LN_EPS) * g + b


def _ret_proj_kernel(x_ref, w_ref, c_ref, s_ref, o_ref, *, tn):
    xb = x_ref[...].astype(BF16)
    c_q, s_q = c_ref[...], s_ref[...]
    k_scale = RET_QK_DIM ** -0.5
    c_k, s_k = c_q * k_scale, s_q * k_scale
    for j in range(w_ref.shape[1] // tn):
        at = j * tn
        acc = jnp.dot(xb, w_ref[:, at:at + tn], preferred_element_type=F32)
        if at >= 2 * D_MODEL:
            o_ref[:, at:at + tn] = acc.astype(BF16)
            continue
        c, s = (c_q, s_q) if at < D_MODEL else (c_k, s_k)
        for h in range(tn // (2 * LANES)):
            ev = 2 * h * LANES
            od = ev + LANES
            e, o = acc[:, ev:ev + LANES], acc[:, od:od + LANES]
            o_ref[:, at + ev:at + ev + LANES] = (e * c - o * s).astype(BF16)
            o_ref[:, at + od:at + od + LANES] = (o * c + e * s).astype(BF16)


def _ret_project(x, w_stack, layer, cos, sin, *, seq, tm=512, tn=1024):
    m, kdim = x.shape
    n = w_stack.shape[2]
    nseq = seq // tm
    return pl.pallas_call(
        functools.partial(_ret_proj_kernel, tn=tn),
        out_shape=jax.ShapeDtypeStruct((m, n), BF16),
        grid=(m // tm,),
        in_specs=[
            pl.BlockSpec((tm, kdim), lambda i: (i, 0)),
            _resident_layer((kdim, n), layer),
            pl.BlockSpec((tm, LANES), lambda i: (i % nseq, 0)),
            pl.BlockSpec((tm, LANES), lambda i: (i % nseq, 0)),
        ],
        out_specs=pl.BlockSpec((tm, n), lambda i: (i, 0)),
        compiler_params=_params("parallel"),
        name="ret_proj",
    )(x, w_stack, cos, sin)


def _ret_kernel(q_ref, k_ref, v_ref, g_ref, dec_ref, xi_ref, zeta_ref, gc_ref, o_ref,
                r_ref, s_scr, kv_scr, rb_scr, o_scr):
    @pl.when(pl.program_id(2) == 0)
    def _():
        r_ref[...] = jnp.zeros_like(r_ref)

    c_sz = RET_CHUNK
    n_chunks = q_ref.shape[0] // c_sz
    for c in range(n_chunks):
        rows = slice(c * c_sz, (c + 1) * c_sz)
        s = lax.dot_general(q_ref[rows, :], k_ref[rows, :], (((1,), (1,)), ((), ())),
                            preferred_element_type=F32)
        s_scr[c] = (s * dec_ref[...]).astype(BF16)
        kz = (k_ref[rows, :].astype(F32) * zeta_ref[...]).astype(BF16)
        kv_scr[c] = lax.dot_general(kz, v_ref[rows, :], (((0,), (0,)), ((), ())),
                                    preferred_element_type=F32)
    r = r_ref[...]
    for c in range(n_chunks):
        rb_scr[c] = r.astype(BF16)
        r = gc_ref[...] * r + kv_scr[c]
    r_ref[...] = r
    for c in range(n_chunks):
        rows = slice(c * c_sz, (c + 1) * c_sz)
        o = jnp.dot(s_scr[c], v_ref[rows, :], preferred_element_type=F32)
        o_scr[rows, :] = o + jnp.dot(q_ref[rows, :], rb_scr[c], preferred_element_type=F32) * xi_ref[...]
    o = o_scr[...]
    of = o * lax.rsqrt(jnp.mean(o * o, axis=-1, keepdims=True) + GN_EPS)
    gate = g_ref[...].astype(F32)
    o_ref[...] = (gate * jax.nn.sigmoid(gate) * of).astype(BF16)


def _retention(proj, dec, xi, zeta, gc, *, batch, seq, tc=1024):
    ns = seq // tc
    n_chunks = tc // RET_CHUNK
    qb = RET_QK_DIM
    vb = RET_V_DIM
    row = lambda b, i: b * ns + i
    return pl.pallas_call(
        _ret_kernel,
        out_shape=jax.ShapeDtypeStruct((batch * seq, RET_VW), BF16),
        grid=(batch, RET_HEADS, ns),
        in_specs=[
            pl.BlockSpec((tc, qb), lambda b, h, i: (row(b, i), h)),
            pl.BlockSpec((tc, qb), lambda b, h, i: (row(b, i), D_MODEL // qb + h)),
            pl.BlockSpec((tc, vb), lambda b, h, i: (row(b, i), 2 * D_MODEL // vb + h)),
            pl.BlockSpec((tc, vb), lambda b, h, i: (row(b, i), (2 * D_MODEL + RET_VW) // vb + h)),
            pl.BlockSpec((None, RET_CHUNK, RET_CHUNK), lambda b, h, i: (h, 0, 0)),
            pl.BlockSpec((None, RET_CHUNK, vb), lambda b, h, i: (h, 0, 0)),
            pl.BlockSpec((None, RET_CHUNK, qb), lambda b, h, i: (h, 0, 0)),
            pl.BlockSpec((None, 1, vb), lambda b, h, i: (h, 0, 0)),
        ],
        out_specs=pl.BlockSpec((tc, vb), lambda b, h, i: (row(b, i), h)),
        scratch_shapes=[pltpu.VMEM((qb, vb), F32),
                        pltpu.VMEM((n_chunks, RET_CHUNK, RET_CHUNK), BF16),
                        pltpu.VMEM((n_chunks, qb, vb), F32),
                        pltpu.VMEM((n_chunks, qb, vb), BF16),
                        pltpu.VMEM((tc, vb), F32)],
        compiler_params=_params("parallel", "parallel", "arbitrary"),
        name="retention",
    )(proj, proj, proj, proj, dec, xi, zeta, gc)


def _finish_layer(mix, x, w_out_ref, ln_ref, wu_ref, wd_ref, o_ref, *, ff_chunk):
    x1 = _layernorm(ALPHA * x + jnp.dot(mix, w_out_ref[...], preferred_element_type=F32),
                    ln_ref[0:1, :], ln_ref[1:2, :])
    xb = x1.astype(BF16)
    y = ALPHA * x1
    for c in range(wu_ref.shape[1] // ff_chunk):
        cols = slice(c * ff_chunk, (c + 1) * ff_chunk)
        h = jnp.maximum(jnp.dot(xb, wu_ref[:, cols], preferred_element_type=F32), 0.0)
        y = y + jnp.dot((h * h).astype(BF16), wd_ref[cols, :], preferred_element_type=F32)
    o_ref[...] = _layernorm(y, ln_ref[2:3, :], ln_ref[3:4, :])


def _ret_tail_kernel(a_ref, x_ref, w_out_ref, ln_ref, wu_ref, wd_ref, o_ref, *, ff_chunk):
    _finish_layer(a_ref[...], x_ref[...], w_out_ref, ln_ref, wu_ref, wd_ref, o_ref, ff_chunk=ff_chunk)


def _dil_tail_kernel(o1_ref, o2_ref, o3_ref, l1_ref, l2_ref, l3_ref, x_ref, w_out_ref, ln_ref, wu_ref, wd_ref,
                     o_ref, mix_a, mix_b, *, ff_chunk):
    s = pl.program_id(0)

    @pl.when(s == 0)
    def _():
        mix_b[...] = jnp.zeros_like(mix_b)

    def step(cur, prev):
        l1, l2, l3 = l1_ref[...], l2_ref[...], l3_ref[...]
        m = jnp.maximum(jnp.maximum(l1, l2), l3)
        e1, e2, e3 = jnp.exp(l1 - m), jnp.exp(l2 - m), jnp.exp(l3 - m)
        inv = 1.0 / (e1 + e2 + e3)
        w1, w2 = e1 * inv, e2 * inv
        for h in range(DIL_HEADS):
            cols = slice(h * DIL_HEAD_DIM, (h + 1) * DIL_HEAD_DIM)
            o3 = o3_ref[:, cols].astype(F32)
            comb = (o3 + w1[:, h:h + 1] * (o1_ref[:, cols].astype(F32) - o3)
                    + w2[:, h:h + 1] * (o2_ref[:, cols].astype(F32) - o3))
            cur[:, cols] = comb.astype(BF16)
        _finish_layer(prev[...], x_ref[...], w_out_ref, ln_ref, wu_ref, wd_ref, o_ref, ff_chunk=ff_chunk)

    @pl.when(s % 2 == 0)
    def _():
        step(mix_a, mix_b)

    @pl.when(s % 2 == 1)
    def _():
        step(mix_b, mix_a)


def _layer_tail(mixes, lses, x, w_out_stack, mixer_layer, ln, wu_stack, wd_stack, layer, *, tm=512, ff_chunk=1024):
    m = x.shape[0]
    n = m // tm
    dilated = len(mixes) > 1
    kdim = w_out_stack.shape[1]
    if dilated:
        ahead = lambda width: pl.BlockSpec((tm, width), lambda s: (jnp.minimum(s, n - 1), 0))
        tile = lambda width: pl.BlockSpec((tm, width), lambda s: (jnp.maximum(s - 1, 0), 0))
    else:
        ahead = tile = lambda width: pl.BlockSpec((tm, width), lambda i: (i, 0))
    in_specs = ([ahead(a.shape[1]) for a in mixes] + [ahead(LANES)] * len(lses) + [
        tile(D_MODEL), _resident_layer((kdim, D_MODEL), mixer_layer), _resident((4, D_MODEL)),
        _resident_layer((D_MODEL, D_FF), layer), _resident_layer((D_FF, D_MODEL), layer)])
    kern = _dil_tail_kernel if dilated else _ret_tail_kernel
    return pl.pallas_call(
        functools.partial(kern, ff_chunk=ff_chunk),
        out_shape=jax.ShapeDtypeStruct((m, D_MODEL), F32),
        grid=(n + 1 if dilated else n,),
        in_specs=in_specs,
        out_specs=tile(D_MODEL),
        scratch_shapes=[pltpu.VMEM((tm, D_MODEL), BF16)] * 2 if dilated else [],
        compiler_params=_params("arbitrary"),
        name="dil_tail" if dilated else "ret_tail",
    )(*mixes, *lses, x, w_out_stack, ln, wu_stack, wd_stack)


def _dil_proj_kernel(*refs, units, tm):
    nb = N_LANE_BLOCKS
    x_refs = refs[:nb]
    w_ref, cos_ref, sin_ref = refs[nb:nb + 3]
    outs = refs[nb + 3:-2]
    xb_ref, tab_ref = refs[-2:]

    def by_residue(ref, d, r):
        return ref[...] if d == 1 else ref[pl.ds(r, tm // d, stride=d), :]

    for gi, d in enumerate(DILATIONS):
        n = tm // d
        for r in range(d):
            rows = slice(r * n, (r + 1) * n)
            for c in range(nb):
                xb_ref[gi, rows, c * LANES:(c + 1) * LANES] = by_residue(x_refs[c], d, r).astype(BF16)
            tab_ref[gi, 0, rows, :] = by_residue(cos_ref, d, r)
            tab_ref[gi, 1, rows, :] = by_residue(sin_ref, d, r)

    for jj, (rope, scale) in enumerate(units):
        gi = jj % N_GROUPS
        d = DILATIONS[gi]
        n = tm // d
        acc = jnp.dot(xb_ref[gi], w_ref[:, jj * D_MODEL:(jj + 1) * D_MODEL], preferred_element_type=F32)
        for h in range(DIL_HEADS):
            t = acc[:, h * LANES:(h + 1) * LANES]
            if rope:
                t = t * tab_ref[gi, 0] + pltpu.roll(t, LANES // 2, axis=1) * tab_ref[gi, 1]
            if scale != 1.0:
                t = t * scale
            tb = t.astype(BF16)
            for r in range(d):
                outs[jj][:, r * D_MODEL + h * LANES:r * D_MODEL + (h + 1) * LANES] = tb[r * n:(r + 1) * n]


def _dil_project(x, w, tables, units, *, batch, seq, tm=512):
    nseq = seq // tm
    out_shape, out_specs = [], []
    for jj in range(len(units)):
        d = DILATIONS[jj % N_GROUPS]
        out_shape.append(jax.ShapeDtypeStruct((batch, seq // d, d * D_MODEL), BF16))
        out_specs.append(pl.BlockSpec((None, tm // d, d * D_MODEL), lambda i: (i // nseq, i % nseq, 0)))
    x_specs = [pl.BlockSpec((tm, LANES), functools.partial(lambda i, c: (i, c), c=c))
               for c in range(N_LANE_BLOCKS)]
    tab_spec = pl.BlockSpec((tm, LANES), lambda i: (i % nseq, 0))
    return pl.pallas_call(
        functools.partial(_dil_proj_kernel, units=units, tm=tm),
        out_shape=out_shape,
        grid=(x.shape[0] // tm,),
        in_specs=x_specs + [_resident(w.shape), tab_spec, tab_spec],
        out_specs=out_specs,
        scratch_shapes=[pltpu.VMEM((N_GROUPS, tm, D_MODEL), BF16),
                        pltpu.VMEM((N_GROUPS, 2, tm, LANES), F32)],
        compiler_params=_params("parallel"),
        name="dil_proj",
    )(*([x] * N_LANE_BLOCKS), w, *tables)


def _dil_attn_kernel(q_ref, kc_ref, vc_ref, kp_ref, vp_ref, o_ref, lse_ref, s_scr, p_scr, *acc, dilation):
    blk = DIL_BLK
    d = dilation
    r = pl.program_id(2)
    no_prev = (pl.program_id(1) == 0).astype(jnp.int32)
    row = lax.broadcasted_iota(jnp.int32, (blk, blk), 0)
    col = lax.broadcasted_iota(jnp.int32, (blk, blk), 1)
    cur_ok = col <= row
    first_prev_ok = col >= row + no_prev * blk
    row2 = lax.broadcasted_iota(jnp.int32, (blk, 2 * blk), 0)
    col2 = lax.broadcasted_iota(jnp.int32, (blk, 2 * blk), 1)
    band_ok = jnp.logical_and(col2 >= row2, col2 <= row2 + blk)
    lane = lax.broadcasted_iota(jnp.int32, (blk, LANES), 1)
    nt = (((1,), (1,)), ((), ()))
    for c in range(q_ref.shape[0] // blk):
        rows = slice(c * blk, (c + 1) * blk)
        win = slice((c - 1) * blk, (c + 1) * blk)
        tok = rows if d == 1 else pl.ds(c * blk * d + r, blk, stride=d)
        for h in range(DIL_HEADS):
            cols = slice(h * DIL_HEAD_DIM, (h + 1) * DIL_HEAD_DIM)
            qh = q_ref[rows, cols]
            if c == 0:
                s_p = lax.dot_general(qh, kp_ref[:, cols], nt, preferred_element_type=F32)
                s_c = lax.dot_general(qh, kc_ref[rows, cols], nt, preferred_element_type=F32)
                s_scr[c, h, :, :blk] = jnp.where(first_prev_ok, s_p, MASKED)
                s_scr[c, h, :, blk:] = jnp.where(cur_ok, s_c, MASKED)
            else:
                s = lax.dot_general(qh, kc_ref[win, cols], nt, preferred_element_type=F32)
                s_scr[c, h] = jnp.where(band_ok, s, MASKED)
        s_all = s_scr[c]
        m = jnp.max(s_all, axis=-1, keepdims=True)
        e = jnp.exp(s_all - m)
        l = jnp.sum(e, axis=-1, keepdims=True)
        p_scr[c] = e.astype(BF16)
        inv = 1.0 / l
        lse = m + jnp.log(l)
        lse_tile = jnp.zeros((blk, LANES), F32)
        for h in range(DIL_HEADS):
            cols = slice(h * DIL_HEAD_DIM, (h + 1) * DIL_HEAD_DIM)
            if c == 0:
                o = jnp.dot(p_scr[c, h, :, :blk], vp_ref[:, cols], preferred_element_type=F32)
                o = o + jnp.dot(p_scr[c, h, :, blk:], vc_ref[rows, cols], preferred_element_type=F32)
            else:
                o = jnp.dot(p_scr[c, h], vc_ref[win, cols], preferred_element_type=F32)
            o = o * inv[h]
            if d == 1:
                o_ref[rows, cols] = o.astype(BF16)
            else:
                acc[0][h, tok, :] = o
            lse_tile = jnp.where(lane == h, lse[h], lse_tile)
        lse_ref[tok, :] = lse_tile

    if d > 1:
        @pl.when(r == d - 1)
        def _():
            for h in range(DIL_HEADS):
                o_ref[:, h * DIL_HEAD_DIM:(h + 1) * DIL_HEAD_DIM] = acc[0][h].astype(BF16)


def _dilated_attention(q, k, v, dilation, *, batch, seq, tq):
    d = dilation
    n_sub = seq // d
    nq = n_sub // tq
    per = tq // DIL_BLK
    rows_out = tq * d
    cur = pl.BlockSpec((None, tq, D_MODEL), lambda b, i, r: (b, i, r))
    prev = pl.BlockSpec((None, DIL_BLK, D_MODEL), lambda b, i, r: (b, jnp.maximum(i * per - 1, 0), r))
    scratch = [pltpu.VMEM((per, DIL_HEADS, DIL_BLK, 2 * DIL_BLK), F32),
               pltpu.VMEM((per, DIL_HEADS, DIL_BLK, 2 * DIL_BLK), BF16)]
    if d > 1:
        scratch.append(pltpu.VMEM((DIL_HEADS, rows_out, DIL_HEAD_DIM), F32))
    return pl.pallas_call(
        functools.partial(_dil_attn_kernel, dilation=d),
        out_shape=(jax.ShapeDtypeStruct((batch * seq, D_MODEL), BF16),
                   jax.ShapeDtypeStruct((batch * seq, LANES), F32)),
        grid=(batch, nq, d),
        in_specs=[cur, cur, cur, prev, prev],
        out_specs=(pl.BlockSpec((rows_out, D_MODEL), lambda b, i, r: (b * nq + i, 0)),
                   pl.BlockSpec((rows_out, LANES), lambda b, i, r: (b * nq + i, 0))),
        scratch_shapes=scratch,
        compiler_params=_params("parallel", "arbitrary", "arbitrary"),
        name=f"dil_attn_d{d}",
    )(q, k, v, k, v)


def _retention_tables(seq):
    half = RET_QK_DIM // 2
    angle = 1.0 / (RET_THETA ** jnp.linspace(0.0, 1.0, half, dtype=F32))
    ang = jnp.arange(seq, dtype=F32)[:, None] * angle[None]
    c_sz = RET_CHUNK
    log_g = jnp.log(1.0 - 2.0 ** (-5.0 - jnp.arange(RET_HEADS, dtype=F32)))
    idx = jnp.arange(c_sz, dtype=F32)
    diff = idx[:, None] - idx[None, :]
    decay = jnp.where(diff[None] >= 0, jnp.exp(jnp.maximum(diff, 0.0)[None] * log_g[:, None, None]), 0.0)
    xi = jnp.exp((idx[None] + 1.0) * log_g[:, None])
    zeta = jnp.exp((c_sz - 1.0 - idx[None]) * log_g[:, None])
    g_chunk = jnp.exp(c_sz * log_g)
    xi_b = jnp.broadcast_to(xi[:, :, None], (RET_HEADS, c_sz, RET_V_DIM))
    zeta_b = jnp.broadcast_to(zeta[:, :, None], (RET_HEADS, c_sz, RET_QK_DIM))
    gc_b = jnp.broadcast_to(g_chunk[:, None, None], (RET_HEADS, 1, RET_V_DIM))
    return jnp.cos(ang), jnp.sin(ang), decay, xi_b, zeta_b, gc_b


def _rope_tables(seq):
    half = ROT_DIMS // 2
    inv_freq = ROPE_THETA ** (-jnp.arange(0, ROT_DIMS, 2, dtype=F32) / ROT_DIMS)
    ang = jnp.arange(seq, dtype=F32)[:, None] * inv_freq[None]
    cos, sin = jnp.cos(ang), jnp.sin(ang)
    gap = LANES // 2 - half
    ones = jnp.ones((seq, gap), F32)
    zeros = jnp.zeros((seq, gap), F32)
    return (jnp.concatenate([cos, ones, cos, ones], axis=1),
            jnp.concatenate([-sin, zeros, sin, zeros], axis=1))


def _pair_split_cols(w, dim):
    lead, n = w.shape[:-1], w.shape[-1]
    return w.reshape(*lead, n // dim, dim // 2, 2).swapaxes(-1, -2).reshape(*lead, n)


def _rope_cols(w):
    k, n = w.shape
    half = ROT_DIMS // 2
    gap = LANES // 2 - half
    wh = w.reshape(k, n // DIL_HEAD_DIM, DIL_HEAD_DIM)
    wh = jnp.concatenate([wh[:, :, :half], wh[:, :, ROT_DIMS:ROT_DIMS + gap],
                          wh[:, :, half:ROT_DIMS], wh[:, :, ROT_DIMS + gap:]], axis=2)
    return wh.reshape(k, n)


def kernel(x, ret_w_in, ret_w_out, kv_w, dil_w_q, dil_w_out, mlp_w_up, mlp_w_down, ln_g, ln_b):
    batch, seq, _ = x.shape
    xf = x.reshape(batch * seq, D_MODEL)

    cos_r, sin_r, decay, xi_b, zeta_b, gc_b = _retention_tables(seq)
    dil_tm = 512
    rope_tabs = _rope_tables(seq)
    attn_tq = (512, 512, 256)

    n_qk = 2 * D_MODEL
    n_k = N_GROUPS * D_MODEL
    q_scale = DIL_HEAD_DIM ** -0.5
    k_units = ((True, 1.0),) * N_GROUPS
    v_units = ((False, 1.0),) * N_GROUPS
    q_units = ((True, q_scale),) * N_GROUPS

    w_in = ret_w_in.astype(BF16)
    w_in = jnp.concatenate([_pair_split_cols(w_in[:, :, :n_qk], RET_QK_DIM), w_in[:, :, n_qk:]], axis=2)
    kv_w_b = kv_w.astype(BF16)
    dil_w_q_b = dil_w_q.astype(BF16)
    ret_w_out_b = ret_w_out.astype(BF16)
    dil_w_out_b = dil_w_out.astype(BF16)
    w_up = mlp_w_up.astype(BF16)
    w_down = mlp_w_down.astype(BF16)
    ln = jnp.stack([ln_g[:, 0], ln_b[:, 0], ln_g[:, 1], ln_b[:, 1]], axis=1)

    ks = vs = qs = None
    for l in range(DEPTH):
        if l < N_A:
            proj = _ret_project(xf, w_in, l, cos_r, sin_r, seq=seq)
            mix = _retention(proj, decay, xi_b, zeta_b, gc_b, batch=batch, seq=seq)
            xf = _layer_tail([mix], [], xf, ret_w_out_b, l, ln[l], w_up, w_down, l)
        else:
            if l > N_A:
                w_q = _rope_cols(dil_w_q_b[l - N_A])
                qs = _dil_project(xf, w_q, rope_tabs, q_units, batch=batch, seq=seq, tm=dil_tm)
            outs, lses = [], []
            for gi, d in enumerate(DILATIONS):
                o, lse = _dilated_attention(qs[gi], ks[gi], vs[gi], d, batch=batch, seq=seq, tq=attn_tq[gi])
                outs.append(o)
                lses.append(lse)
            xf = _layer_tail(outs, lses, xf, dil_w_out_b, l - N_A, ln[l], w_up, w_down, l)
        if l == N_A - 1:
            w_kvq = jnp.concatenate([_rope_cols(kv_w_b[:, :n_k]), kv_w_b[:, n_k:],
                                     _rope_cols(dil_w_q_b[0])], axis=1)
            kvq = _dil_project(xf, w_kvq, rope_tabs, k_units + v_units + q_units,
                               batch=batch, seq=seq, tm=dil_tm)
            ks, vs, qs = kvq[:N_GROUPS], kvq[N_GROUPS:2 * N_GROUPS], kvq[2 * N_GROUPS:]
    return xf.reshape(batch, seq, D_MODEL)
```

```python
import functools

import jax
import jax.numpy as jnp
from jax import lax
from jax.experimental import pallas as pl
from jax.experimental.pallas import tpu as pltpu

D_MODEL = 1024
DEPTH = 4
N_A = DEPTH // 2

RET_HEADS = 4
RET_QK_DIM = D_MODEL // RET_HEADS
RET_V_DIM = 2 * RET_QK_DIM
RET_CHUNK = 256
RET_THETA = 10000.0
RET_VW = RET_HEADS * RET_V_DIM

DIL_GROUPS = ((128, 1), (512, 4), (2048, 16))
DILATIONS = tuple(d for _, d in DIL_GROUPS)
N_GROUPS = len(DIL_GROUPS)
DIL_HEADS = 8
DIL_HEAD_DIM = D_MODEL // DIL_HEADS
ROT_DIMS = DIL_HEAD_DIM // 4
ROPE_THETA = 500000.0
DIL_BLK = 128

D_FF = 4 * D_MODEL
ALPHA = (2.0 * DEPTH) ** 0.25
LN_EPS = 1e-5
GN_EPS = 1e-6

LANES = 128
N_LANE_BLOCKS = D_MODEL // LANES
VMEM_LIMIT = 56 * 1024 * 1024
MASKED = -1e30

BF16 = jnp.bfloat16
F32 = jnp.float32


def _params(*sem):
    return pltpu.CompilerParams(dimension_semantics=sem, vmem_limit_bytes=VMEM_LIMIT)


def _resident(shape):
    return pl.BlockSpec(shape, lambda *_: (0,) * len(shape), pipeline_mode=pl.Buffered(1))


def _resident_layer(shape, layer):
    return pl.BlockSpec((None,) + tuple(shape), lambda *_: (layer,) + (0,) * len(shape),
                        pipeline_mode=pl.Buffered(1))


def _layernorm(y, g, b):
    mu = jnp.mean(y, axis=-1, keepdims=True)
    yc = y - mu
    var = jnp.mean(yc * yc, axis=-1, keepdims=True)
    return yc * lax.rsqrt(var + LN_EPS) * g + b


def _ret_proj_kernel(x_ref, w_ref, c_ref, s_ref, o_ref, *, tn):
    xb = x_ref[...].astype(BF16)
    c_q, s_q = c_ref[...], s_ref[...]
    k_scale = RET_QK_DIM ** -0.5
    c_k, s_k = c_q * k_scale, s_q * k_scale
    for j in range(w_ref.shape[1] // tn):
        at = j * tn
        acc = jnp.dot(xb, w_ref[:, at:at + tn], preferred_element_type=F32)
        if at >= 2 * D_MODEL:
            o_ref[:, at:at + tn] = acc.astype(BF16)
            continue
        c, s = (c_q, s_q) if at < D_MODEL else (c_k, s_k)
        for h in range(tn // (2 * LANES)):
            ev = 2 * h * LANES
            od = ev + LANES
            e, o = acc[:, ev:ev + LANES], acc[:, od:od + LANES]
            o_ref[:, at + ev:at + ev + LANES] = (e * c - o * s).astype(BF16)
            o_ref[:, at + od:at + od + LANES] = (o * c + e * s).astype(BF16)


def _ret_project(x, w_stack, layer, cos, sin, *, seq, tm=512, tn=1024):
    m, kdim = x.shape
    n = w_stack.shape[2]
    nseq = seq // tm
    return pl.pallas_call(
        functools.partial(_ret_proj_kernel, tn=tn),
        out_shape=jax.ShapeDtypeStruct((m, n), BF16),
        grid=(m // tm,),
        in_specs=[
            pl.BlockSpec((tm, kdim), lambda i: (i, 0)),
            _resident_layer((kdim, n), layer),
            pl.BlockSpec((tm, LANES), lambda i: (i % nseq, 0)),
            pl.BlockSpec((tm, LANES), lambda i: (i % nseq, 0)),
        ],
        out_specs=pl.BlockSpec((tm, n), lambda i: (i, 0)),
        compiler_params=_params("parallel"),
        name="ret_proj",
    )(x, w_stack, cos, sin)


def _ret_kernel(q_ref, k_ref, v_ref, g_ref, dec_ref, xi_ref, zeta_ref, gc_ref, o_ref,
                r_ref, s_scr, kv_scr, rb_scr, o_scr):
    @pl.when(pl.program_id(2) == 0)
    def _():
        r_ref[...] = jnp.zeros_like(r_ref)

    c_sz = RET_CHUNK
    n_chunks = q_ref.shape[0] // c_sz
    for c in range(n_chunks):
        rows = slice(c * c_sz, (c + 1) * c_sz)
        s = lax.dot_general(q_ref[rows, :], k_ref[rows, :], (((1,), (1,)), ((), ())),
                            preferred_element_type=F32)
        s_scr[c] = (s * dec_ref[...]).astype(BF16)
        kz = (k_ref[rows, :].astype(F32) * zeta_ref[...]).astype(BF16)
        kv_scr[c] = lax.dot_general(kz, v_ref[rows, :], (((0,), (0,)), ((), ())),
                                    preferred_element_type=F32)
    r = r_ref[...]
    for c in range(n_chunks):
        rb_scr[c] = r.astype(BF16)
        r = gc_ref[...] * r + kv_scr[c]
    r_ref[...] = r
    for c in range(n_chunks):
        rows = slice(c * c_sz, (c + 1) * c_sz)
        o = jnp.dot(s_scr[c], v_ref[rows, :], preferred_element_type=F32)
        o_scr[rows, :] = o + jnp.dot(q_ref[rows, :], rb_scr[c], preferred_element_type=F32) * xi_ref[...]
    o = o_scr[...]
    of = o * lax.rsqrt(jnp.mean(o * o, axis=-1, keepdims=True) + GN_EPS)
    gate = g_ref[...].astype(F32)
    o_ref[...] = (gate * jax.nn.sigmoid(gate) * of).astype(BF16)


def _retention(proj, dec, xi, zeta, gc, *, batch, seq, tc=2048):
    ns = seq // tc
    n_chunks = tc // RET_CHUNK
    qb = RET_QK_DIM
    vb = RET_V_DIM
    row = lambda b, i: b * ns + i
    return pl.pallas_call(
        _ret_kernel,
        out_shape=jax.ShapeDtypeStruct((batch * seq, RET_VW), BF16),
        grid=(batch, RET_HEADS, ns),
        in_specs=[
            pl.BlockSpec((tc, qb), lambda b, h, i: (row(b, i), h)),
            pl.BlockSpec((tc, qb), lambda b, h, i: (row(b, i), D_MODEL // qb + h)),
            pl.BlockSpec((tc, vb), lambda b, h, i: (row(b, i), 2 * D_MODEL // vb + h)),
            pl.BlockSpec((tc, vb), lambda b, h, i: (row(b, i), (2 * D_MODEL + RET_VW) // vb + h)),
            pl.BlockSpec((None, RET_CHUNK, RET_CHUNK), lambda b, h, i: (h, 0, 0)),
            pl.BlockSpec((None, RET_CHUNK, vb), lambda b, h, i: (h, 0, 0)),
            pl.BlockSpec((None, RET_CHUNK, qb), lambda b, h, i: (h, 0, 0)),
            pl.BlockSpec((None, 1, vb), lambda b, h, i: (h, 0, 0)),
        ],
        out_specs=pl.BlockSpec((tc, vb), lambda b, h, i: (row(b, i), h)),
        scratch_shapes=[pltpu.VMEM((qb, vb), F32),
                        pltpu.VMEM((n_chunks, RET_CHUNK, RET_CHUNK), BF16),
                        pltpu.VMEM((n_chunks, qb, vb), F32),
                        pltpu.VMEM((n_chunks, qb, vb), BF16),
                        pltpu.VMEM((tc, vb), F32)],
        compiler_params=_params("parallel", "parallel", "arbitrary"),
        name="retention",
    )(proj, proj, proj, proj, dec, xi, zeta, gc)


def _finish_layer(mix, x, w_out_ref, ln_ref, wu_ref, wd_ref, o_ref, *, ff_chunk):
    x1 = _layernorm(ALPHA * x + jnp.dot(mix, w_out_ref[...], preferred_element_type=F32),
                    ln_ref[0:1, :], ln_ref[1:2, :])
    xb = x1.astype(BF16)
    y = ALPHA * x1
    for c in range(wu_ref.shape[1] // ff_chunk):
        cols = slice(c * ff_chunk, (c + 1) * ff_chunk)
        h = jnp.maximum(jnp.dot(xb, wu_ref[:, cols], preferred_element_type=F32), 0.0)
        y = y + jnp.dot((h * h).astype(BF16), wd_ref[cols, :], preferred_element_type=F32)
    o_ref[...] = _layernorm(y, ln_ref[2:3, :], ln_ref[3:4, :])


def _ret_tail_kernel(a_ref, x_ref, w_out_ref, ln_ref, wu_ref, wd_ref, o_ref, *, ff_chunk):
    _finish_layer(a_ref[...], x_ref[...], w_out_ref, ln_ref, wu_ref, wd_ref, o_ref, ff_chunk=ff_chunk)


def _dil_tail_kernel(o1_ref, o2_ref, o3_ref, l1_ref, l2_ref, l3_ref, x_ref, w_out_ref, ln_ref, wu_ref, wd_ref,
                     o_ref, mix_a, mix_b, *, ff_chunk):
    s = pl.program_id(0)

    @pl.when(s == 0)
    def _():
        mix_b[...] = jnp.zeros_like(mix_b)

    def step(cur, prev):
        l1, l2, l3 = l1_ref[...], l2_ref[...], l3_ref[...]
        m = jnp.maximum(jnp.maximum(l1, l2), l3)
        e1, e2, e3 = jnp.exp(l1 - m), jnp.exp(l2 - m), jnp.exp(l3 - m)
        inv = 1.0 / (e1 + e2 + e3)
        w1, w2 = e1 * inv, e2 * inv
        for h in range(DIL_HEADS):
            cols = slice(h * DIL_HEAD_DIM, (h + 1) * DIL_HEAD_DIM)
            o3 = o3_ref[:, cols].astype(F32)
            comb = (o3 + w1[:, h:h + 1] * (o1_ref[:, cols].astype(F32) - o3)
                    + w2[:, h:h + 1] * (o2_ref[:, cols].astype(F32) - o3))
            cur[:, cols] = comb.astype(BF16)
        _finish_layer(prev[...], x_ref[...], w_out_ref, ln_ref, wu_ref, wd_ref, o_ref, ff_chunk=ff_chunk)

    @pl.when(s % 2 == 0)
    def _():
        step(mix_a, mix_b)

    @pl.when(s % 2 == 1)
    def _():
        step(mix_b, mix_a)


def _layer_tail(mixes, lses, x, w_out_stack, mixer_layer, ln, wu_stack, wd_stack, layer, *, tm=512, ff_chunk=1024):
    m = x.shape[0]
    n = m // tm
    dilated = len(mixes) > 1
    kdim = w_out_stack.shape[1]
    if dilated:
        ahead = lambda width: pl.BlockSpec((tm, width), lambda s: (jnp.minimum(s, n - 1), 0))
        tile = lambda width: pl.BlockSpec((tm, width), lambda s: (jnp.maximum(s - 1, 0), 0))
    else:
        ahead = tile = lambda width: pl.BlockSpec((tm, width), lambda i: (i, 0))
    in_specs = ([ahead(a.shape[1]) for a in mixes] + [ahead(LANES)] * len(lses) + [
        tile(D_MODEL), _resident_layer((kdim, D_MODEL), mixer_layer), _resident((4, D_MODEL)),
        _resident_layer((D_MODEL, D_FF), layer), _resident_layer((D_FF, D_MODEL), layer)])
    kern = _dil_tail_kernel if dilated else _ret_tail_kernel
    return pl.pallas_call(
        functools.partial(kern, ff_chunk=ff_chunk),
        out_shape=jax.ShapeDtypeStruct((m, D_MODEL), F32),
        grid=(n + 1 if dilated else n,),
        in_specs=in_specs,
        out_specs=tile(D_MODEL),
        scratch_shapes=[pltpu.VMEM((tm, D_MODEL), BF16)] * 2 if dilated else [],
        compiler_params=_params("arbitrary"),
        name="dil_tail" if dilated else "ret_tail",
    )(*mixes, *lses, x, w_out_stack, ln, wu_stack, wd_stack)


def _dil_proj_kernel(*refs, units, tm):
    nb = N_LANE_BLOCKS
    x_refs = refs[:nb]
    w_ref, cos_ref, sin_ref = refs[nb:nb + 3]
    outs = refs[nb + 3:-2]
    xb_ref, tab_ref = refs[-2:]

    def by_residue(ref, d, r):
        return ref[...] if d == 1 else ref[pl.ds(r, tm // d, stride=d), :]

    for gi, d in enumerate(DILATIONS):
        n = tm // d
        for r in range(d):
            rows = slice(r * n, (r + 1) * n)
            for c in range(nb):
                xb_ref[gi, rows, c * LANES:(c + 1) * LANES] = by_residue(x_refs[c], d, r).astype(BF16)
            tab_ref[gi, 0, rows, :] = by_residue(cos_ref, d, r)
            tab_ref[gi, 1, rows, :] = by_residue(sin_ref, d, r)

    for jj, (rope, scale) in enumerate(units):
        gi = jj % N_GROUPS
        d = DILATIONS[gi]
        n = tm // d
        acc = jnp.dot(xb_ref[gi], w_ref[:, jj * D_MODEL:(jj + 1) * D_MODEL], preferred_element_type=F32)
        for h in range(DIL_HEADS):
            t = acc[:, h * LANES:(h + 1) * LANES]
            if rope:
                t = t * tab_ref[gi, 0] + pltpu.roll(t, LANES // 2, axis=1) * tab_ref[gi, 1]
            if scale != 1.0:
                t = t * scale
            tb = t.astype(BF16)
            for r in range(d):
                outs[jj][:, r * D_MODEL + h * LANES:r * D_MODEL + (h + 1) * LANES] = tb[r * n:(r + 1) * n]


def _dil_project(x, w, tables, units, *, batch, seq, tm=512):
    nseq = seq // tm
    out_shape, out_specs = [], []
    for jj in range(len(units)):
        d = DILATIONS[jj % N_GROUPS]
        out_shape.append(jax.ShapeDtypeStruct((batch, seq // d, d * D_MODEL), BF16))
        out_specs.append(pl.BlockSpec((None, tm // d, d * D_MODEL), lambda i: (i // nseq, i % nseq, 0)))
    x_specs = [pl.BlockSpec((tm, LANES), functools.partial(lambda i, c: (i, c), c=c))
               for c in range(N_LANE_BLOCKS)]
    tab_spec = pl.BlockSpec((tm, LANES), lambda i: (i % nseq, 0))
    return pl.pallas_call(
        functools.partial(_dil_proj_kernel, units=units, tm=tm),
        out_shape=out_shape,
        grid=(x.shape[0] // tm,),
        in_specs=x_specs + [_resident(w.shape), tab_spec, tab_spec],
        out_specs=out_specs,
        scratch_shapes=[pltpu.VMEM((N_GROUPS, tm, D_MODEL), BF16),
                        pltpu.VMEM((N_GROUPS, 2, tm, LANES), F32)],
        compiler_params=_params("parallel"),
        name="dil_proj",
    )(*([x] * N_LANE_BLOCKS), w, *tables)


def _dil_attn_kernel(q_ref, kc_ref, vc_ref, kp_ref, vp_ref, o_ref, lse_ref, s_scr, p_scr, *acc, dilation):
    blk = DIL_BLK
    d = dilation
    r = pl.program_id(2)
    no_prev = (pl.program_id(1) == 0).astype(jnp.int32)
    row = lax.broadcasted_iota(jnp.int32, (blk, blk), 0)
    col = lax.broadcasted_iota(jnp.int32, (blk, blk), 1)
    cur_ok = col <= row
    first_prev_ok = col >= row + no_prev * blk
    row2 = lax.broadcasted_iota(jnp.int32, (blk, 2 * blk), 0)
    col2 = lax.broadcasted_iota(jnp.int32, (blk, 2 * blk), 1)
    band_ok = jnp.logical_and(col2 >= row2, col2 <= row2 + blk)
    lane = lax.broadcasted_iota(jnp.int32, (blk, LANES), 1)
    nt = (((1,), (1,)), ((), ()))
    for c in range(q_ref.shape[0] // blk):
        rows = slice(c * blk, (c + 1) * blk)
        win = slice((c - 1) * blk, (c + 1) * blk)
        tok = rows if d == 1 else pl.ds(c * blk * d + r, blk, stride=d)
        for h in range(DIL_HEADS):
            cols = slice(h * DIL_HEAD_DIM, (h + 1) * DIL_HEAD_DIM)
            qh = q_ref[rows, cols]
            if c == 0:
                s_p = lax.dot_general(qh, kp_ref[:, cols], nt, preferred_element_type=F32)
                s_c = lax.dot_general(qh, kc_ref[rows, cols], nt, preferred_element_type=F32)
                s_scr[c, h, :, :blk] = jnp.where(first_prev_ok, s_p, MASKED)
                s_scr[c, h, :, blk:] = jnp.where(cur_ok, s_c, MASKED)
            else:
                s = lax.dot_general(qh, kc_ref[win, cols], nt, preferred_element_type=F32)
                s_scr[c, h] = jnp.where(band_ok, s, MASKED)
        s_all = s_scr[c]
        m = jnp.max(s_all, axis=-1, keepdims=True)
        e = jnp.exp(s_all - m)
        l = jnp.sum(e, axis=-1, keepdims=True)
        p_scr[c] = e.astype(BF16)
        inv = 1.0 / l
        lse = m + jnp.log(l)
        lse_tile = jnp.zeros((blk, LANES), F32)
        for h in range(DIL_HEADS):
            cols = slice(h * DIL_HEAD_DIM, (h + 1) * DIL_HEAD_DIM)
            if c == 0:
                o = jnp.dot(p_scr[c, h, :, :blk], vp_ref[:, cols], preferred_element_type=F32)
                o = o + jnp.dot(p_scr[c, h, :, blk:], vc_ref[rows, cols], preferred_element_type=F32)
            else:
                o = jnp.dot(p_scr[c, h], vc_ref[win, cols], preferred_element_type=F32)
            o = o * inv[h]
            if d == 1:
                o_ref[rows, cols] = o.astype(BF16)
            else:
                acc[0][h, tok, :] = o
            lse_tile = jnp.where(lane == h, lse[h], lse_tile)
        lse_ref[tok, :] = lse_tile

    if d > 1:
        @pl.when(r == d - 1)
        def _():
            for h in range(DIL_HEADS):
                o_ref[:, h * DIL_HEAD_DIM:(h + 1) * DIL_HEAD_DIM] = acc[0][h].astype(BF16)


def _dil_attn_staged_kernel(q_ref, kc_ref, vc_ref, kp_ref, vp_ref, o_ref, lse_ref,
                            s_scr, p_scr, kt_scr, v_scr, *acc, dilation, rps):
    blk = DIL_BLK
    d = dilation
    pid = pl.program_id(2)
    no_prev = (pl.program_id(1) == 0).astype(jnp.int32)
    row2 = lax.broadcasted_iota(jnp.int32, (blk, 2 * blk), 0)
    col2 = lax.broadcasted_iota(jnp.int32, (blk, 2 * blk), 1)
    band_ok = jnp.logical_and(col2 >= row2, col2 <= row2 + blk)
    first_band_ok = jnp.logical_and(col2 >= row2 * (1 - no_prev) + no_prev * blk, col2 <= row2 + blk)
    lane = lax.broadcasted_iota(jnp.int32, (blk, LANES), 1)
    n_blk = q_ref.shape[0] // blk
    for rr in range(rps):
        base = rr * D_MODEL
        r = pid * rps + rr
        v_scr[rr, :blk, :] = vp_ref[:, base:base + D_MODEL]
        v_scr[rr, blk:, :] = vc_ref[:, base:base + D_MODEL]
        for h in range(DIL_HEADS):
            cols = slice(base + h * DIL_HEAD_DIM, base + (h + 1) * DIL_HEAD_DIM)
            kt_scr[rr, h, :, :blk] = kp_ref[:, cols].T
            for c in range(n_blk):
                kt_scr[rr, h, :, (c + 1) * blk:(c + 2) * blk] = kc_ref[c * blk:(c + 1) * blk, cols].T
        for c in range(n_blk):
            rows = slice(c * blk, (c + 1) * blk)
            win = slice(c * blk, (c + 2) * blk)
            tok = rows if d == 1 else pl.ds(c * blk * d + r, blk, stride=d)
            for h in range(DIL_HEADS):
                cols = slice(base + h * DIL_HEAD_DIM, base + (h + 1) * DIL_HEAD_DIM)
                s = jnp.dot(q_ref[rows, cols], kt_scr[rr, h, :, win], preferred_element_type=F32)
                s_scr[rr, c, h] = jnp.where(band_ok if c else first_band_ok, s, MASKED)
            s_all = s_scr[rr, c]
            m = jnp.max(s_all, axis=-1, keepdims=True)
            e = jnp.exp(s_all - m)
            l = jnp.sum(e, axis=-1, keepdims=True)
            p_scr[rr, c] = e.astype(BF16)
            inv = 1.0 / l
            lse = m + jnp.log(l)
            lse_tile = jnp.zeros((blk, LANES), F32)
            for h in range(DIL_HEADS):
                hcols = slice(h * DIL_HEAD_DIM, (h + 1) * DIL_HEAD_DIM)
                o = jnp.dot(p_scr[rr, c, h], v_scr[rr, win, hcols], preferred_element_type=F32) * inv[h]
                if d == 1:
                    o_ref[rows, hcols] = o.astype(BF16)
                else:
                    acc[0][h, tok, :] = o
                lse_tile = jnp.where(lane == h, lse[h], lse_tile)
            lse_ref[tok, :] = lse_tile

    if d > 1:
        @pl.when(pid == d // rps - 1)
        def _():
            for h in range(DIL_HEADS):
                o_ref[:, h * DIL_HEAD_DIM:(h + 1) * DIL_HEAD_DIM] = acc[0][h].astype(BF16)


def _dilated_attention(q, k, v, dilation, *, batch, seq, tq, rps=0):
    d = dilation
    n_sub = seq // d
    nq = n_sub // tq
    per = tq // DIL_BLK
    rows_out = tq * d
    width = max(rps, 1) * D_MODEL
    cur = pl.BlockSpec((None, tq, width), lambda b, i, r: (b, i, r))
    prev = pl.BlockSpec((None, DIL_BLK, width), lambda b, i, r: (b, jnp.maximum(i * per - 1, 0), r))
    if rps:
        kern = functools.partial(_dil_attn_staged_kernel, dilation=d, rps=rps)
        scratch = [pltpu.VMEM((rps, per, DIL_HEADS, DIL_BLK, 2 * DIL_BLK), F32),
                   pltpu.VMEM((rps, per, DIL_HEADS, DIL_BLK, 2 * DIL_BLK), BF16),
                   pltpu.VMEM((rps, DIL_HEADS, DIL_HEAD_DIM, tq + DIL_BLK), BF16),
                   pltpu.VMEM((rps, tq + DIL_BLK, D_MODEL), BF16)]
    else:
        kern = functools.partial(_dil_attn_kernel, dilation=d)
        scratch = [pltpu.VMEM((per, DIL_HEADS, DIL_BLK, 2 * DIL_BLK), F32),
                   pltpu.VMEM((per, DIL_HEADS, DIL_BLK, 2 * DIL_BLK), BF16)]
    if d > 1:
        scratch.append(pltpu.VMEM((DIL_HEADS, rows_out, DIL_HEAD_DIM), F32))
    return pl.pallas_call(
        kern,
        out_shape=(jax.ShapeDtypeStruct((batch * seq, D_MODEL), BF16),
                   jax.ShapeDtypeStruct((batch * seq, LANES), F32)),
        grid=(batch, nq, d // max(rps, 1)),
        in_specs=[cur, cur, cur, prev, prev],
        out_specs=(pl.BlockSpec((rows_out, D_MODEL), lambda b, i, r: (b * nq + i, 0)),
                   pl.BlockSpec((rows_out, LANES), lambda b, i, r: (b * nq + i, 0))),
        scratch_shapes=scratch,
        compiler_params=_params("parallel", "arbitrary", "arbitrary"),
        name=f"dil_attn_d{d}" + (f"_s{rps}" if rps else ""),
    )(q, k, v, k, v)


def _retention_tables(seq):
    half = RET_QK_DIM // 2
    angle = 1.0 / (RET_THETA ** jnp.linspace(0.0, 1.0, half, dtype=F32))
    ang = jnp.arange(seq, dtype=F32)[:, None] * angle[None]
    c_sz = RET_CHUNK
    log_g = jnp.log(1.0 - 2.0 ** (-5.0 - jnp.arange(RET_HEADS, dtype=F32)))
    idx = jnp.arange(c_sz, dtype=F32)
    diff = idx[:, None] - idx[None, :]
    decay = jnp.where(diff[None] >= 0, jnp.exp(jnp.maximum(diff, 0.0)[None] * log_g[:, None, None]), 0.0)
    xi = jnp.exp((idx[None] + 1.0) * log_g[:, None])
    zeta = jnp.exp((c_sz - 1.0 - idx[None]) * log_g[:, None])
    g_chunk = jnp.exp(c_sz * log_g)
    xi_b = jnp.broadcast_to(xi[:, :, None], (RET_HEADS, c_sz, RET_V_DIM))
    zeta_b = jnp.broadcast_to(zeta[:, :, None], (RET_HEADS, c_sz, RET_QK_DIM))
    gc_b = jnp.broadcast_to(g_chunk[:, None, None], (RET_HEADS, 1, RET_V_DIM))
    return jnp.cos(ang), jnp.sin(ang), decay, xi_b, zeta_b, gc_b


def _rope_tables(seq):
    half = ROT_DIMS // 2
    inv_freq = ROPE_THETA ** (-jnp.arange(0, ROT_DIMS, 2, dtype=F32) / ROT_DIMS)
    ang = jnp.arange(seq, dtype=F32)[:, None] * inv_freq[None]
    cos, sin = jnp.cos(ang), jnp.sin(ang)
    gap = LANES // 2 - half
    ones = jnp.ones((seq, gap), F32)
    zeros = jnp.zeros((seq, gap), F32)
    return (jnp.concatenate([cos, ones, cos, ones], axis=1),
            jnp.concatenate([-sin, zeros, sin, zeros], axis=1))


def _pair_split_cols(w, dim):
    lead, n = w.shape[:-1], w.shape[-1]
    return w.reshape(*lead, n // dim, dim // 2, 2).swapaxes(-1, -2).reshape(*lead, n)


def _rope_cols(w):
    k, n = w.shape
    half = ROT_DIMS // 2
    gap = LANES // 2 - half
    wh = w.reshape(k, n // DIL_HEAD_DIM, DIL_HEAD_DIM)
    wh = jnp.concatenate([wh[:, :, :half], wh[:, :, ROT_DIMS:ROT_DIMS + gap],
                          wh[:, :, half:ROT_DIMS], wh[:, :, ROT_DIMS + gap:]], axis=2)
    return wh.reshape(k, n)


def kernel(x, ret_w_in, ret_w_out, kv_w, dil_w_q, dil_w_out, mlp_w_up, mlp_w_down, ln_g, ln_b):
    batch, seq, _ = x.shape
    xf = x.reshape(batch * seq, D_MODEL)

    cos_r, sin_r, decay, xi_b, zeta_b, gc_b = _retention_tables(seq)
    dil_tm = 512
    rope_tabs = _rope_tables(seq)
    attn_tq = (512, 512, 256)
    attn_rps = (1, 1, 2)

    n_qk = 2 * D_MODEL
    n_k = N_GROUPS * D_MODEL
    q_scale = DIL_HEAD_DIM ** -0.5
    k_units = ((True, 1.0),) * N_GROUPS
    v_units = ((False, 1.0),) * N_GROUPS
    q_units = ((True, q_scale),) * N_GROUPS

    w_in = ret_w_in.astype(BF16)
    w_in = jnp.concatenate([_pair_split_cols(w_in[:, :, :n_qk], RET_QK_DIM), w_in[:, :, n_qk:]], axis=2)
    kv_w_b = kv_w.astype(BF16)
    dil_w_q_b = dil_w_q.astype(BF16)
    ret_w_out_b = ret_w_out.astype(BF16)
    dil_w_out_b = dil_w_out.astype(BF16)
    w_up = mlp_w_up.astype(BF16)
    w_down = mlp_w_down.astype(BF16)
    ln = jnp.stack([ln_g[:, 0], ln_b[:, 0], ln_g[:, 1], ln_b[:, 1]], axis=1)

    ks = vs = qs = None
    for l in range(DEPTH):
        if l < N_A:
            proj = _ret_project(xf, w_in, l, cos_r, sin_r, seq=seq)
            mix = _retention(proj, decay, xi_b, zeta_b, gc_b, batch=batch, seq=seq)
            xf = _layer_tail([mix], [], xf, ret_w_out_b, l, ln[l], w_up, w_down, l)
        else:
            if l > N_A:
                w_q = _rope_cols(dil_w_q_b[l - N_A])
                qs = _dil_project(xf, w_q, rope_tabs, q_units, batch=batch, seq=seq, tm=dil_tm)
            outs, lses = [], []
            for gi, d in enumerate(DILATIONS):
                o, lse = _dilated_attention(qs[gi], ks[gi], vs[gi], d, batch=batch, seq=seq, tq=attn_tq[gi],
                                            rps=attn_rps[gi] if l > N_A else 0)
                outs.append(o)
                lses.append(lse)
            xf = _layer_tail(outs, lses, xf, dil_w_out_b, l - N_A, ln[l], w_up, w_down, l)
        if l == N_A - 1:
            w_kvq = jnp.concatenate([_rope_cols(kv_w_b[:, :n_k]), kv_w_b[:, n_k:],
                                     _rope_cols(dil_w_q_b[0])], axis=1)
            kvq = _dil_project(xf, w_kvq, rope_tabs, k_units + v_units + q_units,
                               batch=batch, seq=seq, tm=dil_tm)
            ks, vs, qs = kvq[:N_GROUPS], kvq[N_GROUPS:2 * N_GROUPS], kvq[2 * N_GROUPS:]
    return xf.reshape(batch, seq, D_MODEL)
```

```python
import functools

import jax
import jax.numpy as jnp
from jax import lax
from jax.experimental import pallas as pl
from jax.experimental.pallas import tpu as pltpu

D_MODEL = 1024
DEPTH = 4
N_A = DEPTH // 2

RET_HEADS = 4
RET_QK_DIM = D_MODEL // RET_HEADS
RET_V_DIM = 2 * RET_QK_DIM
RET_CHUNK = 256
RET_THETA = 10000.0
RET_VW = RET_HEADS * RET_V_DIM

DIL_GROUPS = ((128, 1), (512, 4), (2048, 16))
DILATIONS = tuple(d for _, d in DIL_GROUPS)
N_GROUPS = len(DIL_GROUPS)
DIL_HEADS = 8
DIL_HEAD_DIM = D_MODEL // DIL_HEADS
ROT_DIMS = DIL_HEAD_DIM // 4
ROPE_THETA = 500000.0
DIL_BLK = 128

D_FF = 4 * D_MODEL
ALPHA = (2.0 * DEPTH) ** 0.25
LN_EPS = 1e-5
GN_EPS = 1e-6

LANES = 128
N_LANE_BLOCKS = D_MODEL // LANES
VMEM_LIMIT = 56 * 1024 * 1024
MASKED = -1e30

BF16 = jnp.bfloat16
F32 = jnp.float32


def _params(*sem):
    return pltpu.CompilerParams(dimension_semantics=sem, vmem_limit_bytes=VMEM_LIMIT)


def _resident(shape):
    return pl.BlockSpec(shape, lambda *_: (0,) * len(shape), pipeline_mode=pl.Buffered(1))


def _resident_layer(shape, layer):
    return pl.BlockSpec((None,) + tuple(shape), lambda *_: (layer,) + (0,) * len(shape),
                        pipeline_mode=pl.Buffered(1))


def _layernorm(y, g, b):
    mu = jnp.mean(y, axis=-1, keepdims=True)
    yc = y - mu
    var = jnp.mean(yc * yc, axis=-1, keepdims=True)
    return yc * lax.rsqrt(var + LN_EPS) * g + b


def _ret_proj_kernel(x_ref, w_ref, c_ref, s_ref, o_ref, *, tn):
    xb = x_ref[...].astype(BF16)
    c_q, s_q = c_ref[...], s_ref[...]
    k_scale = RET_QK_DIM ** -0.5
    c_k, s_k = c_q * k_scale, s_q * k_scale
    for j in range(w_ref.shape[1] // tn):
        at = j * tn
        acc = jnp.dot(xb, w_ref[:, at:at + tn], preferred_element_type=F32)
        if at >= 2 * D_MODEL:
            o_ref[:, at:at + tn] = acc.astype(BF16)
            continue
        c, s = (c_q, s_q) if at < D_MODEL else (c_k, s_k)
        for h in range(tn // (2 * LANES)):
            ev = 2 * h * LANES
            od = ev + LANES
            e, o = acc[:, ev:ev + LANES], acc[:, od:od + LANES]
            o_ref[:, at + ev:at + ev + LANES] = (e * c - o * s).astype(BF16)
            o_ref[:, at + od:at + od + LANES] = (o * c + e * s).astype(BF16)


def _ret_project(x, w_stack, layer, cos, sin, *, seq, tm=1024, tn=1024):
    m, kdim = x.shape
    n = w_stack.shape[2]
    nseq = seq // tm
    return pl.pallas_call(
        functools.partial(_ret_proj_kernel, tn=tn),
        out_shape=jax.ShapeDtypeStruct((m, n), BF16),
        grid=(m // tm,),
        in_specs=[
            pl.BlockSpec((tm, kdim), lambda i: (i, 0)),
            _resident_layer((kdim, n), layer),
            pl.BlockSpec((tm, LANES), lambda i: (i % nseq, 0)),
            pl.BlockSpec((tm, LANES), lambda i: (i % nseq, 0)),
        ],
        out_specs=pl.BlockSpec((tm, n), lambda i: (i, 0)),
        compiler_params=_params("parallel"),
        name="ret_proj",
    )(x, w_stack, cos, sin)


def _ret_kernel(q_ref, k_ref, v_ref, g_ref, dec_ref, xi_ref, zeta_ref, gc_ref, o_ref,
                r_ref, s_scr, kv_scr, rb_scr, o_scr):
    @pl.when(pl.program_id(2) == 0)
    def _():
        r_ref[...] = jnp.zeros_like(r_ref)

    c_sz = RET_CHUNK
    n_chunks = q_ref.shape[0] // c_sz
    for h in range(r_ref.shape[0]):
        qk = slice(h * RET_QK_DIM, (h + 1) * RET_QK_DIM)
        vg = slice(h * RET_V_DIM, (h + 1) * RET_V_DIM)
        for c in range(n_chunks):
            rows = slice(c * c_sz, (c + 1) * c_sz)
            s = lax.dot_general(q_ref[rows, qk], k_ref[rows, qk], (((1,), (1,)), ((), ())),
                                preferred_element_type=F32)
            s_scr[c] = (s * dec_ref[h]).astype(BF16)
            kz = (k_ref[rows, qk].astype(F32) * zeta_ref[h]).astype(BF16)
            kv_scr[c] = lax.dot_general(kz, v_ref[rows, vg], (((0,), (0,)), ((), ())),
                                        preferred_element_type=F32)
        r = r_ref[h]
        for c in range(n_chunks):
            rb_scr[c] = r.astype(BF16)
            r = gc_ref[h] * r + kv_scr[c]
        r_ref[h] = r
        for c in range(n_chunks):
            rows = slice(c * c_sz, (c + 1) * c_sz)
            o = jnp.dot(s_scr[c], v_ref[rows, vg], preferred_element_type=F32)
            o_scr[rows, :] = o + jnp.dot(q_ref[rows, qk], rb_scr[c], preferred_element_type=F32) * xi_ref[h]
        o = o_scr[...]
        of = o * lax.rsqrt(jnp.mean(o * o, axis=-1, keepdims=True) + GN_EPS)
        gate = g_ref[:, vg].astype(F32)
        o_ref[:, vg] = (gate * jax.nn.sigmoid(gate) * of).astype(BF16)


def _retention(proj, dec, xi, zeta, gc, *, batch, seq, tc=2048, hps=2):
    ns = seq // tc
    n_chunks = tc // RET_CHUNK
    qb = hps * RET_QK_DIM
    vb = hps * RET_V_DIM
    row = lambda b, i: b * ns + i
    per_head = lambda width: pl.BlockSpec((hps, RET_CHUNK, width), lambda b, h, i: (h, 0, 0))
    return pl.pallas_call(
        _ret_kernel,
        out_shape=jax.ShapeDtypeStruct((batch * seq, RET_VW), BF16),
        grid=(batch, RET_HEADS // hps, ns),
        in_specs=[
            pl.BlockSpec((tc, qb), lambda b, h, i: (row(b, i), h)),
            pl.BlockSpec((tc, qb), lambda b, h, i: (row(b, i), D_MODEL // qb + h)),
            pl.BlockSpec((tc, vb), lambda b, h, i: (row(b, i), 2 * D_MODEL // vb + h)),
            pl.BlockSpec((tc, vb), lambda b, h, i: (row(b, i), (2 * D_MODEL + RET_VW) // vb + h)),
            per_head(RET_CHUNK), per_head(RET_V_DIM), per_head(RET_QK_DIM),
            pl.BlockSpec((hps, 1, RET_V_DIM), lambda b, h, i: (h, 0, 0)),
        ],
        out_specs=pl.BlockSpec((tc, vb), lambda b, h, i: (row(b, i), h)),
        scratch_shapes=[pltpu.VMEM((hps, RET_QK_DIM, RET_V_DIM), F32),
                        pltpu.VMEM((n_chunks, RET_CHUNK, RET_CHUNK), BF16),
                        pltpu.VMEM((n_chunks, RET_QK_DIM, RET_V_DIM), F32),
                        pltpu.VMEM((n_chunks, RET_QK_DIM, RET_V_DIM), BF16),
                        pltpu.VMEM((tc, RET_V_DIM), F32)],
        compiler_params=_params("parallel", "parallel", "arbitrary"),
        name="retention",
    )(proj, proj, proj, proj, dec, xi, zeta, gc)


def _finish_layer(mix, x, w_out_ref, ln_ref, wu_ref, wd_ref, o_ref, *, ff_chunk):
    x1 = _layernorm(ALPHA * x + jnp.dot(mix, w_out_ref[...], preferred_element_type=F32),
                    ln_ref[0:1, :], ln_ref[1:2, :])
    xb = x1.astype(BF16)
    y = ALPHA * x1
    for c in range(wu_ref.shape[1] // ff_chunk):
        cols = slice(c * ff_chunk, (c + 1) * ff_chunk)
        h = jnp.maximum(jnp.dot(xb, wu_ref[:, cols], preferred_element_type=F32), 0.0)
        y = y + jnp.dot((h * h).astype(BF16), wd_ref[cols, :], preferred_element_type=F32)
    o_ref[...] = _layernorm(y, ln_ref[2:3, :], ln_ref[3:4, :])


def _ret_tail_kernel(a_ref, x_ref, w_out_ref, ln_ref, wu_ref, wd_ref, o_ref, *, ff_chunk):
    _finish_layer(a_ref[...], x_ref[...], w_out_ref, ln_ref, wu_ref, wd_ref, o_ref, ff_chunk=ff_chunk)


def _dil_tail_kernel(o1_ref, o2_ref, o3_ref, l1_ref, l2_ref, l3_ref, x_ref, w_out_ref, ln_ref, wu_ref, wd_ref,
                     o_ref, mix_a, mix_b, *, ff_chunk):
    s = pl.program_id(0)

    @pl.when(s == 0)
    def _():
        mix_b[...] = jnp.zeros_like(mix_b)

    def step(cur, prev):
        l1, l2, l3 = l1_ref[...], l2_ref[...], l3_ref[...]
        m = jnp.maximum(jnp.maximum(l1, l2), l3)
        e1, e2, e3 = jnp.exp(l1 - m), jnp.exp(l2 - m), jnp.exp(l3 - m)
        inv = 1.0 / (e1 + e2 + e3)
        w1, w2 = e1 * inv, e2 * inv
        for h in range(DIL_HEADS):
            cols = slice(h * DIL_HEAD_DIM, (h + 1) * DIL_HEAD_DIM)
            o3 = o3_ref[:, cols].astype(F32)
            comb = (o3 + w1[:, h:h + 1] * (o1_ref[:, cols].astype(F32) - o3)
                    + w2[:, h:h + 1] * (o2_ref[:, cols].astype(F32) - o3))
            cur[:, cols] = comb.astype(BF16)
        _finish_layer(prev[...], x_ref[...], w_out_ref, ln_ref, wu_ref, wd_ref, o_ref, ff_chunk=ff_chunk)

    @pl.when(s % 2 == 0)
    def _():
        step(mix_a, mix_b)

    @pl.when(s % 2 == 1)
    def _():
        step(mix_b, mix_a)


def _layer_tail(mixes, lses, x, w_out_stack, mixer_layer, ln, wu_stack, wd_stack, layer, *, tm=512, ff_chunk=1024):
    m = x.shape[0]
    n = m // tm
    dilated = len(mixes) > 1
    kdim = w_out_stack.shape[1]
    if dilated:
        ahead = lambda width: pl.BlockSpec((tm, width), lambda s: (jnp.minimum(s, n - 1), 0))
        tile = lambda width: pl.BlockSpec((tm, width), lambda s: (jnp.maximum(s - 1, 0), 0))
    else:
        ahead = tile = lambda width: pl.BlockSpec((tm, width), lambda i: (i, 0))
    in_specs = ([ahead(a.shape[1]) for a in mixes] + [ahead(LANES)] * len(lses) + [
        tile(D_MODEL), _resident_layer((kdim, D_MODEL), mixer_layer), _resident((4, D_MODEL)),
        _resident_layer((D_MODEL, D_FF), layer), _resident_layer((D_FF, D_MODEL), layer)])
    kern = _dil_tail_kernel if dilated else _ret_tail_kernel
    return pl.pallas_call(
        functools.partial(kern, ff_chunk=ff_chunk),
        out_shape=jax.ShapeDtypeStruct((m, D_MODEL), F32),
        grid=(n + 1 if dilated else n,),
        in_specs=in_specs,
        out_specs=tile(D_MODEL),
        scratch_shapes=[pltpu.VMEM((tm, D_MODEL), BF16)] * 2 if dilated else [],
        compiler_params=_params("arbitrary"),
        name="dil_tail" if dilated else "ret_tail",
    )(*mixes, *lses, x, w_out_stack, ln, wu_stack, wd_stack)


def _dil_proj_kernel(*refs, units, tm):
    nb = N_LANE_BLOCKS
    x_refs = refs[:nb]
    w_ref, cos_ref, sin_ref = refs[nb:nb + 3]
    outs = refs[nb + 3:-2]
    xb_ref, tab_ref = refs[-2:]

    def by_residue(ref, d, r):
        return ref[...] if d == 1 else ref[pl.ds(r, tm // d, stride=d), :]

    for gi, d in enumerate(DILATIONS):
        n = tm // d
        for r in range(d):
            rows = slice(r * n, (r + 1) * n)
            for c in range(nb):
                xb_ref[gi, rows, c * LANES:(c + 1) * LANES] = by_residue(x_refs[c], d, r).astype(BF16)
            tab_ref[gi, 0, rows, :] = by_residue(cos_ref, d, r)
            tab_ref[gi, 1, rows, :] = by_residue(sin_ref, d, r)

    for jj, (rope, scale) in enumerate(units):
        gi = jj % N_GROUPS
        d = DILATIONS[gi]
        n = tm // d
        acc = jnp.dot(xb_ref[gi], w_ref[:, jj * D_MODEL:(jj + 1) * D_MODEL], preferred_element_type=F32)
        for h in range(DIL_HEADS):
            t = acc[:, h * LANES:(h + 1) * LANES]
            if rope:
                t = t * tab_ref[gi, 0] + pltpu.roll(t, LANES // 2, axis=1) * tab_ref[gi, 1]
            if scale != 1.0:
                t = t * scale
            tb = t.astype(BF16)
            for r in range(d):
                outs[jj][:, r * D_MODEL + h * LANES:r * D_MODEL + (h + 1) * LANES] = tb[r * n:(r + 1) * n]


def _dil_project(x, w, tables, units, *, batch, seq, tm=512):
    nseq = seq // tm
    out_shape, out_specs = [], []
    for jj in range(len(units)):
        d = DILATIONS[jj % N_GROUPS]
        out_shape.append(jax.ShapeDtypeStruct((batch, seq // d, d * D_MODEL), BF16))
        out_specs.append(pl.BlockSpec((None, tm // d, d * D_MODEL), lambda i: (i // nseq, i % nseq, 0)))
    x_specs = [pl.BlockSpec((tm, LANES), functools.partial(lambda i, c: (i, c), c=c))
               for c in range(N_LANE_BLOCKS)]
    tab_spec = pl.BlockSpec((tm, LANES), lambda i: (i % nseq, 0))
    return pl.pallas_call(
        functools.partial(_dil_proj_kernel, units=units, tm=tm),
        out_shape=out_shape,
        grid=(x.shape[0] // tm,),
        in_specs=x_specs + [_resident(w.shape), tab_spec, tab_spec],
        out_specs=out_specs,
        scratch_shapes=[pltpu.VMEM((N_GROUPS, tm, D_MODEL), BF16),
                        pltpu.VMEM((N_GROUPS, 2, tm, LANES), F32)],
        compiler_params=_params("parallel"),
        name="dil_proj",
    )(*([x] * N_LANE_BLOCKS), w, *tables)


def _dil_attn_kernel(q_ref, kc_ref, vc_ref, kp_ref, vp_ref, o_ref, lse_ref,
                     s_scr, p_scr, kt_scr, v_scr, *acc, dilation, rps):
    blk = DIL_BLK
    d = dilation
    pid = pl.program_id(2)
    no_prev = (pl.program_id(1) == 0).astype(jnp.int32)
    row2 = lax.broadcasted_iota(jnp.int32, (blk, 2 * blk), 0)
    col2 = lax.broadcasted_iota(jnp.int32, (blk, 2 * blk), 1)
    band_ok = jnp.logical_and(col2 >= row2, col2 <= row2 + blk)
    first_band_ok = jnp.logical_and(col2 >= row2 * (1 - no_prev) + no_prev * blk, col2 <= row2 + blk)
    lane = lax.broadcasted_iota(jnp.int32, (blk, LANES), 1)
    n_blk = q_ref.shape[0] // blk
    for rr in range(rps):
        base = rr * D_MODEL
        r = pid * rps + rr
        v_scr[rr, :blk, :] = vp_ref[:, base:base + D_MODEL]
        v_scr[rr, blk:, :] = vc_ref[:, base:base + D_MODEL]
        for h in range(DIL_HEADS):
            cols = slice(base + h * DIL_HEAD_DIM, base + (h + 1) * DIL_HEAD_DIM)
            kt_scr[rr, h, :, :blk] = kp_ref[:, cols].T
            for c in range(n_blk):
                kt_scr[rr, h, :, (c + 1) * blk:(c + 2) * blk] = kc_ref[c * blk:(c + 1) * blk, cols].T
        for c in range(n_blk):
            rows = slice(c * blk, (c + 1) * blk)
            win = slice(c * blk, (c + 2) * blk)
            tok = rows if d == 1 else pl.ds(c * blk * d + r, blk, stride=d)
            for h in range(DIL_HEADS):
                cols = slice(base + h * DIL_HEAD_DIM, base + (h + 1) * DIL_HEAD_DIM)
                s = jnp.dot(q_ref[rows, cols], kt_scr[rr, h, :, win], preferred_element_type=F32)
                s_scr[rr, c, h] = jnp.where(band_ok if c else first_band_ok, s, MASKED)
            s_all = s_scr[rr, c]
            m = jnp.max(s_all, axis=-1, keepdims=True)
            e = jnp.exp(s_all - m)
            l = jnp.sum(e, axis=-1, keepdims=True)
            p_scr[rr, c] = e.astype(BF16)
            inv = 1.0 / l
            lse = m + jnp.log(l)
            lse_tile = jnp.zeros((blk, LANES), F32)
            for h in range(DIL_HEADS):
                hcols = slice(h * DIL_HEAD_DIM, (h + 1) * DIL_HEAD_DIM)
                o = jnp.dot(p_scr[rr, c, h], v_scr[rr, win, hcols], preferred_element_type=F32) * inv[h]
                if d == 1:
                    o_ref[rows, hcols] = o.astype(BF16)
                else:
                    acc[0][h, tok, :] = o
                lse_tile = jnp.where(lane == h, lse[h], lse_tile)
            lse_ref[tok, :] = lse_tile

    if d > 1:
        @pl.when(pid == d // rps - 1)
        def _():
            for h in range(DIL_HEADS):
                o_ref[:, h * DIL_HEAD_DIM:(h + 1) * DIL_HEAD_DIM] = acc[0][h].astype(BF16)


def _dilated_attention(q, k, v, dilation, *, batch, seq, tq, rps):
    d = dilation
    n_sub = seq // d
    nq = n_sub // tq
    per = tq // DIL_BLK
    rows_out = tq * d
    width = rps * D_MODEL
    cur = pl.BlockSpec((None, tq, width), lambda b, i, r: (b, i, r))
    prev = pl.BlockSpec((None, DIL_BLK, width), lambda b, i, r: (b, jnp.maximum(i * per - 1, 0), r))
    scratch = [pltpu.VMEM((rps, per, DIL_HEADS, DIL_BLK, 2 * DIL_BLK), F32),
               pltpu.VMEM((rps, per, DIL_HEADS, DIL_BLK, 2 * DIL_BLK), BF16),
               pltpu.VMEM((rps, DIL_HEADS, DIL_HEAD_DIM, tq + DIL_BLK), BF16),
               pltpu.VMEM((rps, tq + DIL_BLK, D_MODEL), BF16)]
    if d > 1:
        scratch.append(pltpu.VMEM((DIL_HEADS, rows_out, DIL_HEAD_DIM), F32))
    return pl.pallas_call(
        functools.partial(_dil_attn_kernel, dilation=d, rps=rps),
        out_shape=(jax.ShapeDtypeStruct((batch * seq, D_MODEL), BF16),
                   jax.ShapeDtypeStruct((batch * seq, LANES), F32)),
        grid=(batch, nq, d // rps),
        in_specs=[cur, cur, cur, prev, prev],
        out_specs=(pl.BlockSpec((rows_out, D_MODEL), lambda b, i, r: (b * nq + i, 0)),
                   pl.BlockSpec((rows_out, LANES), lambda b, i, r: (b * nq + i, 0))),
        scratch_shapes=scratch,
        compiler_params=_params("parallel", "arbitrary", "arbitrary"),
        name=f"dil_attn_d{d}",
    )(q, k, v, k, v)


def _retention_tables(seq):
    half = RET_QK_DIM // 2
    angle = 1.0 / (RET_THETA ** jnp.linspace(0.0, 1.0, half, dtype=F32))
    ang = jnp.arange(seq, dtype=F32)[:, None] * angle[None]
    c_sz = RET_CHUNK
    log_g = jnp.log(1.0 - 2.0 ** (-5.0 - jnp.arange(RET_HEADS, dtype=F32)))
    idx = jnp.arange(c_sz, dtype=F32)
    diff = idx[:, None] - idx[None, :]
    decay = jnp.where(diff[None] >= 0, jnp.exp(jnp.maximum(diff, 0.0)[None] * log_g[:, None, None]), 0.0)
    xi = jnp.exp((idx[None] + 1.0) * log_g[:, None])
    zeta = jnp.exp((c_sz - 1.0 - idx[None]) * log_g[:, None])
    g_chunk = jnp.exp(c_sz * log_g)
    xi_b = jnp.broadcast_to(xi[:, :, None], (RET_HEADS, c_sz, RET_V_DIM))
    zeta_b = jnp.broadcast_to(zeta[:, :, None], (RET_HEADS, c_sz, RET_QK_DIM))
    gc_b = jnp.broadcast_to(g_chunk[:, None, None], (RET_HEADS, 1, RET_V_DIM))
    return jnp.cos(ang), jnp.sin(ang), decay, xi_b, zeta_b, gc_b


def _rope_tables(seq):
    half = ROT_DIMS // 2
    inv_freq = ROPE_THETA ** (-jnp.arange(0, ROT_DIMS, 2, dtype=F32) / ROT_DIMS)
    ang = jnp.arange(seq, dtype=F32)[:, None] * inv_freq[None]
    cos, sin = jnp.cos(ang), jnp.sin(ang)
    gap = LANES // 2 - half
    ones = jnp.ones((seq, gap), F32)
    zeros = jnp.zeros((seq, gap), F32)
    return (jnp.concatenate([cos, ones, cos, ones], axis=1),
            jnp.concatenate([-sin, zeros, sin, zeros], axis=1))


def _pair_split_cols(w, dim):
    lead, n = w.shape[:-1], w.shape[-1]
    return w.reshape(*lead, n // dim, dim // 2, 2).swapaxes(-1, -2).reshape(*lead, n)


def _rope_cols(w):
    k, n = w.shape
    half = ROT_DIMS // 2
    gap = LANES // 2 - half
    wh = w.reshape(k, n // DIL_HEAD_DIM, DIL_HEAD_DIM)
    wh = jnp.concatenate([wh[:, :, :half], wh[:, :, ROT_DIMS:ROT_DIMS + gap],
                          wh[:, :, half:ROT_DIMS], wh[:, :, ROT_DIMS + gap:]], axis=2)
    return wh.reshape(k, n)


def kernel(x, ret_w_in, ret_w_out, kv_w, dil_w_q, dil_w_out, mlp_w_up, mlp_w_down, ln_g, ln_b):
    batch, seq, _ = x.shape
    xf = x.reshape(batch * seq, D_MODEL)

    cos_r, sin_r, decay, xi_b, zeta_b, gc_b = _retention_tables(seq)
    dil_tm = 512
    rope_tabs = _rope_tables(seq)
    attn_tq = (1024, 512, 256)
    attn_rps = (1, 2, 2)

    n_qk = 2 * D_MODEL
    n_k = N_GROUPS * D_MODEL
    q_scale = DIL_HEAD_DIM ** -0.5
    k_units = ((True, 1.0),) * N_GROUPS
    v_units = ((False, 1.0),) * N_GROUPS
    q_units = ((True, q_scale),) * N_GROUPS

    w_in = ret_w_in.astype(BF16)
    w_in = jnp.concatenate([_pair_split_cols(w_in[:, :, :n_qk], RET_QK_DIM), w_in[:, :, n_qk:]], axis=2)
    kv_w_b = kv_w.astype(BF16)
    dil_w_q_b = dil_w_q.astype(BF16)
    ret_w_out_b = ret_w_out.astype(BF16)
    dil_w_out_b = dil_w_out.astype(BF16)
    w_up = mlp_w_up.astype(BF16)
    w_down = mlp_w_down.astype(BF16)
    ln = jnp.stack([ln_g[:, 0], ln_b[:, 0], ln_g[:, 1], ln_b[:, 1]], axis=1)

    ks = vs = qs = None
    for l in range(DEPTH):
        if l < N_A:
            proj = _ret_project(xf, w_in, l, cos_r, sin_r, seq=seq)
            mix = _retention(proj, decay, xi_b, zeta_b, gc_b, batch=batch, seq=seq)
            xf = _layer_tail([mix], [], xf, ret_w_out_b, l, ln[l], w_up, w_down, l)
        else:
            if l > N_A:
                w_q = _rope_cols(dil_w_q_b[l - N_A])
                qs = _dil_project(xf, w_q, rope_tabs, q_units, batch=batch, seq=seq, tm=dil_tm)
            outs, lses = [], []
            for gi, d in enumerate(DILATIONS):
                o, lse = _dilated_attention(qs[gi], ks[gi], vs[gi], d, batch=batch, seq=seq,
                                            tq=attn_tq[gi], rps=attn_rps[gi])
                outs.append(o)
                lses.append(lse)
            xf = _layer_tail(outs, lses, xf, dil_w_out_b, l - N_A, ln[l], w_up, w_down, l)
        if l == N_A - 1:
            w_kvq = jnp.concatenate([_rope_cols(kv_w_b[:, :n_k]), kv_w_b[:, n_k:],
                                     _rope_cols(dil_w_q_b[0])], axis=1)
            kvq = _dil_project(xf, w_kvq, rope_tabs, k_units + v_units + q_units,
                               batch=batch, seq=seq, tm=dil_tm)
            ks, vs, qs = kvq[:N_GROUPS], kvq[N_GROUPS:2 * N_GROUPS], kvq[2 * N_GROUPS:]
    return xf.reshape(batch, seq, D_MODEL)
```

```python
import functools

import jax
import jax.numpy as jnp
from jax import lax
from jax.experimental import pallas as pl
from jax.experimental.pallas import tpu as pltpu

D_MODEL = 1024
DEPTH = 4
N_A = DEPTH // 2

RET_HEADS = 4
RET_QK_DIM = D_MODEL // RET_HEADS
RET_V_DIM = 2 * RET_QK_DIM
RET_CHUNK = 256
RET_THETA = 10000.0
RET_VW = RET_HEADS * RET_V_DIM

DIL_GROUPS = ((128, 1), (512, 4), (2048, 16))
DILATIONS = tuple(d for _, d in DIL_GROUPS)
N_GROUPS = len(DIL_GROUPS)
DIL_HEADS = 8
DIL_HEAD_DIM = D_MODEL // DIL_HEADS
ROT_DIMS = DIL_HEAD_DIM // 4
ROPE_THETA = 500000.0
DIL_BLK = 128

D_FF = 4 * D_MODEL
ALPHA = (2.0 * DEPTH) ** 0.25
LN_EPS = 1e-5
GN_EPS = 1e-6

LANES = 128
N_LANE_BLOCKS = D_MODEL // LANES
VMEM_LIMIT = 56 * 1024 * 1024
MASKED = -1e30

BF16 = jnp.bfloat16
F32 = jnp.float32


def _params(*sem):
    return pltpu.CompilerParams(dimension_semantics=sem, vmem_limit_bytes=VMEM_LIMIT)


def _resident(shape):
    return pl.BlockSpec(shape, lambda *_: (0,) * len(shape), pipeline_mode=pl.Buffered(1))


def _resident_layer(shape, layer):
    return pl.BlockSpec((None,) + tuple(shape), lambda *_: (layer,) + (0,) * len(shape),
                        pipeline_mode=pl.Buffered(1))


def _layernorm(y, g, b):
    mu = jnp.mean(y, axis=-1, keepdims=True)
    yc = y - mu
    var = jnp.mean(yc * yc, axis=-1, keepdims=True)
    return yc * lax.rsqrt(var + LN_EPS) * g + b


def _ret_proj_kernel(x_ref, w_ref, c_ref, s_ref, o_ref, *, tn):
    xb = x_ref[...].astype(BF16)
    c_q, s_q = c_ref[...], s_ref[...]
    k_scale = RET_QK_DIM ** -0.5
    c_k, s_k = c_q * k_scale, s_q * k_scale
    for j in range(w_ref.shape[1] // tn):
        at = j * tn
        acc = jnp.dot(xb, w_ref[:, at:at + tn], preferred_element_type=F32)
        if at >= 2 * D_MODEL:
            o_ref[:, at:at + tn] = acc.astype(BF16)
            continue
        c, s = (c_q, s_q) if at < D_MODEL else (c_k, s_k)
        for h in range(tn // (2 * LANES)):
            ev = 2 * h * LANES
            od = ev + LANES
            e, o = acc[:, ev:ev + LANES], acc[:, od:od + LANES]
            o_ref[:, at + ev:at + ev + LANES] = (e * c - o * s).astype(BF16)
            o_ref[:, at + od:at + od + LANES] = (o * c + e * s).astype(BF16)


def _ret_project(x, w_stack, layer, cos, sin, *, seq, tm=1024, tn=1024):
    m, kdim = x.shape
    n = w_stack.shape[2]
    nseq = seq // tm
    return pl.pallas_call(
        functools.partial(_ret_proj_kernel, tn=tn),
        out_shape=jax.ShapeDtypeStruct((m, n), BF16),
        grid=(m // tm,),
        in_specs=[
            pl.BlockSpec((tm, kdim), lambda i: (i, 0)),
            _resident_layer((kdim, n), layer),
            pl.BlockSpec((tm, LANES), lambda i: (i % nseq, 0)),
            pl.BlockSpec((tm, LANES), lambda i: (i % nseq, 0)),
        ],
        out_specs=pl.BlockSpec((tm, n), lambda i: (i, 0)),
        compiler_params=_params("parallel"),
        name="ret_proj",
    )(x, w_stack, cos, sin)


def _ret_kernel(q_ref, k_ref, v_ref, g_ref, dec_ref, xi_ref, zeta_ref, gc_ref, o_ref,
                r_ref, s_scr, kv_scr, rb_scr, o_scr):
    @pl.when(pl.program_id(2) == 0)
    def _():
        r_ref[...] = jnp.zeros_like(r_ref)

    c_sz = RET_CHUNK
    n_chunks = q_ref.shape[0] // c_sz
    for h in range(r_ref.shape[0]):
        qk = slice(h * RET_QK_DIM, (h + 1) * RET_QK_DIM)
        vg = slice(h * RET_V_DIM, (h + 1) * RET_V_DIM)
        for c in range(n_chunks):
            rows = slice(c * c_sz, (c + 1) * c_sz)
            s = lax.dot_general(q_ref[rows, qk], k_ref[rows, qk], (((1,), (1,)), ((), ())),
                                preferred_element_type=F32)
            s_scr[c] = (s * dec_ref[h]).astype(BF16)
            kz = (k_ref[rows, qk].astype(F32) * zeta_ref[h]).astype(BF16)
            kv_scr[c] = lax.dot_general(kz, v_ref[rows, vg], (((0,), (0,)), ((), ())),
                                        preferred_element_type=F32)
        r = r_ref[h]
        for c in range(n_chunks):
            rb_scr[c] = r.astype(BF16)
            r = gc_ref[h] * r + kv_scr[c]
        r_ref[h] = r
        for c in range(n_chunks):
            rows = slice(c * c_sz, (c + 1) * c_sz)
            o = jnp.dot(s_scr[c], v_ref[rows, vg], preferred_element_type=F32)
            o_scr[rows, :] = o + jnp.dot(q_ref[rows, qk], rb_scr[c], preferred_element_type=F32) * xi_ref[h]
        o = o_scr[...]
        of = o * lax.rsqrt(jnp.mean(o * o, axis=-1, keepdims=True) + GN_EPS)
        gate = g_ref[:, vg].astype(F32)
        o_ref[:, vg] = (gate * jax.nn.sigmoid(gate) * of).astype(BF16)


def _retention(proj, dec, xi, zeta, gc, *, batch, seq, tc=2048, hps=1):
    ns = seq // tc
    n_chunks = tc // RET_CHUNK
    qb = hps * RET_QK_DIM
    vb = hps * RET_V_DIM
    row = lambda b, i: b * ns + i
    per_head = lambda width: pl.BlockSpec((hps, RET_CHUNK, width), lambda b, h, i: (h, 0, 0))
    return pl.pallas_call(
        _ret_kernel,
        out_shape=jax.ShapeDtypeStruct((batch * seq, RET_VW), BF16),
        grid=(batch, RET_HEADS // hps, ns),
        in_specs=[
            pl.BlockSpec((tc, qb), lambda b, h, i: (row(b, i), h)),
            pl.BlockSpec((tc, qb), lambda b, h, i: (row(b, i), D_MODEL // qb + h)),
            pl.BlockSpec((tc, vb), lambda b, h, i: (row(b, i), 2 * D_MODEL // vb + h)),
            pl.BlockSpec((tc, vb), lambda b, h, i: (row(b, i), (2 * D_MODEL + RET_VW) // vb + h)),
            per_head(RET_CHUNK), per_head(RET_V_DIM), per_head(RET_QK_DIM),
            pl.BlockSpec((hps, 1, RET_V_DIM), lambda b, h, i: (h, 0, 0)),
        ],
        out_specs=pl.BlockSpec((tc, vb), lambda b, h, i: (row(b, i), h)),
        scratch_shapes=[pltpu.VMEM((hps, RET_QK_DIM, RET_V_DIM), F32),
                        pltpu.VMEM((n_chunks, RET_CHUNK, RET_CHUNK), BF16),
                        pltpu.VMEM((n_chunks, RET_QK_DIM, RET_V_DIM), F32),
                        pltpu.VMEM((n_chunks, RET_QK_DIM, RET_V_DIM), BF16),
                        pltpu.VMEM((tc, RET_V_DIM), F32)],
        compiler_params=_params("parallel", "parallel", "arbitrary"),
        name="retention",
    )(proj, proj, proj, proj, dec, xi, zeta, gc)


def _finish_layer(mix, x, w_out_ref, ln_ref, wu_ref, wd_ref, o_ref, *, ff_chunk):
    x1 = _layernorm(ALPHA * x + jnp.dot(mix, w_out_ref[...], preferred_element_type=F32),
                    ln_ref[0:1, :], ln_ref[1:2, :])
    xb = x1.astype(BF16)
    y = ALPHA * x1
    for c in range(wu_ref.shape[1] // ff_chunk):
        cols = slice(c * ff_chunk, (c + 1) * ff_chunk)
        h = jnp.maximum(jnp.dot(xb, wu_ref[:, cols], preferred_element_type=F32), 0.0)
        y = y + jnp.dot((h * h).astype(BF16), wd_ref[cols, :], preferred_element_type=F32)
    o_ref[...] = _layernorm(y, ln_ref[2:3, :], ln_ref[3:4, :])


def _ret_tail_kernel(a_ref, x_ref, w_out_ref, ln_ref, wu_ref, wd_ref, o_ref, *, ff_chunk):
    _finish_layer(a_ref[...], x_ref[...], w_out_ref, ln_ref, wu_ref, wd_ref, o_ref, ff_chunk=ff_chunk)


def _dil_tail_kernel(o1_ref, o2_ref, o3_ref, l1_ref, l2_ref, l3_ref, x_ref, w_out_ref, ln_ref, wu_ref, wd_ref,
                     o_ref, mix_a, mix_b, *, ff_chunk):
    s = pl.program_id(0)

    @pl.when(s == 0)
    def _():
        mix_b[...] = jnp.zeros_like(mix_b)

    def step(cur, prev):
        l1, l2, l3 = l1_ref[...], l2_ref[...], l3_ref[...]
        m = jnp.maximum(jnp.maximum(l1, l2), l3)
        e1, e2, e3 = jnp.exp(l1 - m), jnp.exp(l2 - m), jnp.exp(l3 - m)
        inv = 1.0 / (e1 + e2 + e3)
        w1, w2 = e1 * inv, e2 * inv
        for h in range(DIL_HEADS):
            cols = slice(h * DIL_HEAD_DIM, (h + 1) * DIL_HEAD_DIM)
            o3 = o3_ref[:, cols].astype(F32)
            comb = (o3 + w1[:, h:h + 1] * (o1_ref[:, cols].astype(F32) - o3)
                    + w2[:, h:h + 1] * (o2_ref[:, cols].astype(F32) - o3))
            cur[:, cols] = comb.astype(BF16)
        _finish_layer(prev[...], x_ref[...], w_out_ref, ln_ref, wu_ref, wd_ref, o_ref, ff_chunk=ff_chunk)

    @pl.when(s % 2 == 0)
    def _():
        step(mix_a, mix_b)

    @pl.when(s % 2 == 1)
    def _():
        step(mix_b, mix_a)


def _layer_tail(mixes, lses, x, w_out_stack, mixer_layer, ln, wu_stack, wd_stack, layer, *, tm=512, ff_chunk=1024):
    m = x.shape[0]
    n = m // tm
    dilated = len(mixes) > 1
    kdim = w_out_stack.shape[1]
    if dilated:
        ahead = lambda width: pl.BlockSpec((tm, width), lambda s: (jnp.minimum(s, n - 1), 0))
        tile = lambda width: pl.BlockSpec((tm, width), lambda s: (jnp.maximum(s - 1, 0), 0))
    else:
        ahead = tile = lambda width: pl.BlockSpec((tm, width), lambda i: (i, 0))
    in_specs = ([ahead(a.shape[1]) for a in mixes] + [ahead(LANES)] * len(lses) + [
        tile(D_MODEL), _resident_layer((kdim, D_MODEL), mixer_layer), _resident((4, D_MODEL)),
        _resident_layer((D_MODEL, D_FF), layer), _resident_layer((D_FF, D_MODEL), layer)])
    kern = _dil_tail_kernel if dilated else _ret_tail_kernel
    return pl.pallas_call(
        functools.partial(kern, ff_chunk=ff_chunk),
        out_shape=jax.ShapeDtypeStruct((m, D_MODEL), F32),
        grid=(n + 1 if dilated else n,),
        in_specs=in_specs,
        out_specs=tile(D_MODEL),
        scratch_shapes=[pltpu.VMEM((tm, D_MODEL), BF16)] * 2 if dilated else [],
        compiler_params=_params("arbitrary"),
        name="dil_tail" if dilated else "ret_tail",
    )(*mixes, *lses, x, w_out_stack, ln, wu_stack, wd_stack)


def _dil_proj_kernel(*refs, units, tm):
    nb = N_LANE_BLOCKS
    x_refs = refs[:nb]
    w_ref, cos_ref, sin_ref = refs[nb:nb + 3]
    outs = refs[nb + 3:-3]
    xb_ref, tab_ref, tmp_ref = refs[-3:]

    def put(gi, src, rows, val):
        if src < nb:
            xb_ref[gi, rows, src * LANES:(src + 1) * LANES] = val.astype(BF16)
        else:
            tab_ref[gi, src - nb, rows, :] = val

    step = DILATIONS[1]
    assert DILATIONS == (1, step, step * step)
    n1, n2 = tm // step, tm // (step * step)
    for src, ref in enumerate(list(x_refs) + [cos_ref, sin_ref]):
        put(0, src, slice(0, tm), ref[...])
        for a in range(step):
            t = ref[pl.ds(a, n1, stride=step), :]
            put(1, src, slice(a * n1, (a + 1) * n1), t)
            tmp_ref[src, a] = t
        for a in range(step):
            for b in range(step):
                r = step * b + a
                put(2, src, slice(r * n2, (r + 1) * n2), tmp_ref[src, a, pl.ds(b, n2, stride=step), :])

    for jj, (rope, scale) in enumerate(units):
        gi = jj % N_GROUPS
        d = DILATIONS[gi]
        n = tm // d
        acc = jnp.dot(xb_ref[gi], w_ref[:, jj * D_MODEL:(jj + 1) * D_MODEL], preferred_element_type=F32)
        for h in range(DIL_HEADS):
            t = acc[:, h * LANES:(h + 1) * LANES]
            if rope:
                t = t * tab_ref[gi, 0] + pltpu.roll(t, LANES // 2, axis=1) * tab_ref[gi, 1]
            if scale != 1.0:
                t = t * scale
            tb = t.astype(BF16)
            for r in range(d):
                outs[jj][:, r * D_MODEL + h * LANES:r * D_MODEL + (h + 1) * LANES] = tb[r * n:(r + 1) * n]


def _dil_project(x, w, tables, units, *, batch, seq, tm=512):
    nseq = seq // tm
    out_shape, out_specs = [], []
    for jj in range(len(units)):
        d = DILATIONS[jj % N_GROUPS]
        out_shape.append(jax.ShapeDtypeStruct((batch, seq // d, d * D_MODEL), BF16))
        out_specs.append(pl.BlockSpec((None, tm // d, d * D_MODEL), lambda i: (i // nseq, i % nseq, 0)))
    x_specs = [pl.BlockSpec((tm, LANES), functools.partial(lambda i, c: (i, c), c=c))
               for c in range(N_LANE_BLOCKS)]
    tab_spec = pl.BlockSpec((tm, LANES), lambda i: (i % nseq, 0))
    return pl.pallas_call(
        functools.partial(_dil_proj_kernel, units=units, tm=tm),
        out_shape=out_shape,
        grid=(x.shape[0] // tm,),
        in_specs=x_specs + [_resident(w.shape), tab_spec, tab_spec],
        out_specs=out_specs,
        scratch_shapes=[pltpu.VMEM((N_GROUPS, tm, D_MODEL), BF16),
                        pltpu.VMEM((N_GROUPS, 2, tm, LANES), F32),
                        pltpu.VMEM((N_LANE_BLOCKS + 2, DILATIONS[1], tm // DILATIONS[1], LANES), F32)],
        compiler_params=_params("parallel"),
        name="dil_proj",
    )(*([x] * N_LANE_BLOCKS), w, *tables)


def _dil_attn_kernel(q_ref, kc_ref, vc_ref, kp_ref, vp_ref, o_ref, lse_ref,
                     s_scr, p_scr, kt_scr, v_scr, *acc, dilation, rps):
    blk = DIL_BLK
    d = dilation
    pid = pl.program_id(2)
    no_prev = (pl.program_id(1) == 0).astype(jnp.int32)
    row2 = lax.broadcasted_iota(jnp.int32, (blk, 2 * blk), 0)
    col2 = lax.broadcasted_iota(jnp.int32, (blk, 2 * blk), 1)
    band_ok = jnp.logical_and(col2 >= row2, col2 <= row2 + blk)
    first_band_ok = jnp.logical_and(col2 >= row2 * (1 - no_prev) + no_prev * blk, col2 <= row2 + blk)
    lane = lax.broadcasted_iota(jnp.int32, (blk, LANES), 1)
    n_blk = q_ref.shape[0] // blk
    for rr in range(rps):
        base = rr * D_MODEL
        r = pid * rps + rr
        v_scr[rr, :blk, :] = vp_ref[:, base:base + D_MODEL]
        v_scr[rr, blk:, :] = vc_ref[:, base:base + D_MODEL]
        for h in range(DIL_HEADS):
            cols = slice(base + h * DIL_HEAD_DIM, base + (h + 1) * DIL_HEAD_DIM)
            kt_scr[rr, h, :, :blk] = kp_ref[:, cols].T
            for c in range(n_blk):
                kt_scr[rr, h, :, (c + 1) * blk:(c + 2) * blk] = kc_ref[c * blk:(c + 1) * blk, cols].T
        for c in range(n_blk):
            rows = slice(c * blk, (c + 1) * blk)
            win = slice(c * blk, (c + 2) * blk)
            tok = rows if d == 1 else pl.ds(c * blk * d + r, blk, stride=d)
            for h in range(DIL_HEADS):
                cols = slice(base + h * DIL_HEAD_DIM, base + (h + 1) * DIL_HEAD_DIM)
                s = jnp.dot(q_ref[rows, cols], kt_scr[rr, h, :, win], preferred_element_type=F32)
                s_scr[rr, c, h] = jnp.where(band_ok if c else first_band_ok, s, MASKED)
            s_all = s_scr[rr, c]
            m = jnp.max(s_all, axis=-1, keepdims=True)
            e = jnp.exp(s_all - m)
            l = jnp.sum(e, axis=-1, keepdims=True)
            p_scr[rr, c] = e.astype(BF16)
            inv = 1.0 / l
            lse = m + jnp.log(l)
            lse_tile = jnp.zeros((blk, LANES), F32)
            for h in range(DIL_HEADS):
                hcols = slice(h * DIL_HEAD_DIM, (h + 1) * DIL_HEAD_DIM)
                o = jnp.dot(p_scr[rr, c, h], v_scr[rr, win, hcols], preferred_element_type=F32) * inv[h]
                if d == 1:
                    o_ref[rows, hcols] = o.astype(BF16)
                else:
                    acc[0][h, tok, :] = o
                lse_tile = jnp.where(lane == h, lse[h], lse_tile)
            lse_ref[tok, :] = lse_tile

    if d > 1:
        @pl.when(pid == d // rps - 1)
        def _():
            for h in range(DIL_HEADS):
                o_ref[:, h * DIL_HEAD_DIM:(h + 1) * DIL_HEAD_DIM] = acc[0][h].astype(BF16)


def _dilated_attention(q, k, v, dilation, *, batch, seq, tq, rps):
    d = dilation
    n_sub = seq // d
    nq = n_sub // tq
    per = tq // DIL_BLK
    rows_out = tq * d
    width = rps * D_MODEL
    cur = pl.BlockSpec((None, tq, width), lambda b, i, r: (b, i, r))
    prev = pl.BlockSpec((None, DIL_BLK, width), lambda b, i, r: (b, jnp.maximum(i * per - 1, 0), r))
    scratch = [pltpu.VMEM((rps, per, DIL_HEADS, DIL_BLK, 2 * DIL_BLK), F32),
               pltpu.VMEM((rps, per, DIL_HEADS, DIL_BLK, 2 * DIL_BLK), BF16),
               pltpu.VMEM((rps, DIL_HEADS, DIL_HEAD_DIM, tq + DIL_BLK), BF16),
               pltpu.VMEM((rps, tq + DIL_BLK, D_MODEL), BF16)]
    if d > 1:
        scratch.append(pltpu.VMEM((DIL_HEADS, rows_out, DIL_HEAD_DIM), F32))
    return pl.pallas_call(
        functools.partial(_dil_attn_kernel, dilation=d, rps=rps),
        out_shape=(jax.ShapeDtypeStruct((batch * seq, D_MODEL), BF16),
                   jax.ShapeDtypeStruct((batch * seq, LANES), F32)),
        grid=(batch, nq, d // rps),
        in_specs=[cur, cur, cur, prev, prev],
        out_specs=(pl.BlockSpec((rows_out, D_MODEL), lambda b, i, r: (b * nq + i, 0)),
                   pl.BlockSpec((rows_out, LANES), lambda b, i, r: (b * nq + i, 0))),
        scratch_shapes=scratch,
        compiler_params=_params("parallel", "arbitrary", "arbitrary"),
        name=f"dil_attn_d{d}",
    )(q, k, v, k, v)


def _retention_tables(seq):
    half = RET_QK_DIM // 2
    angle = 1.0 / (RET_THETA ** jnp.linspace(0.0, 1.0, half, dtype=F32))
    ang = jnp.arange(seq, dtype=F32)[:, None] * angle[None]
    c_sz = RET_CHUNK
    log_g = jnp.log(1.0 - 2.0 ** (-5.0 - jnp.arange(RET_HEADS, dtype=F32)))
    idx = jnp.arange(c_sz, dtype=F32)
    diff = idx[:, None] - idx[None, :]
    decay = jnp.where(diff[None] >= 0, jnp.exp(jnp.maximum(diff, 0.0)[None] * log_g[:, None, None]), 0.0)
    xi = jnp.exp((idx[None] + 1.0) * log_g[:, None])
    zeta = jnp.exp((c_sz - 1.0 - idx[None]) * log_g[:, None])
    g_chunk = jnp.exp(c_sz * log_g)
    xi_b = jnp.broadcast_to(xi[:, :, None], (RET_HEADS, c_sz, RET_V_DIM))
    zeta_b = jnp.broadcast_to(zeta[:, :, None], (RET_HEADS, c_sz, RET_QK_DIM))
    gc_b = jnp.broadcast_to(g_chunk[:, None, None], (RET_HEADS, 1, RET_V_DIM))
    return jnp.cos(ang), jnp.sin(ang), decay, xi_b, zeta_b, gc_b


def _rope_tables(seq):
    half = ROT_DIMS // 2
    inv_freq = ROPE_THETA ** (-jnp.arange(0, ROT_DIMS, 2, dtype=F32) / ROT_DIMS)
    ang = jnp.arange(seq, dtype=F32)[:, None] * inv_freq[None]
    cos, sin = jnp.cos(ang), jnp.sin(ang)
    gap = LANES // 2 - half
    ones = jnp.ones((seq, gap), F32)
    zeros = jnp.zeros((seq, gap), F32)
    return (jnp.concatenate([cos, ones, cos, ones], axis=1),
            jnp.concatenate([-sin, zeros, sin, zeros], axis=1))


def _pair_split_cols(w, dim):
    lead, n = w.shape[:-1], w.shape[-1]
    return w.reshape(*lead, n // dim, dim // 2, 2).swapaxes(-1, -2).reshape(*lead, n)


def _rope_cols(w):
    k, n = w.shape
    half = ROT_DIMS // 2
    gap = LANES // 2 - half
    wh = w.reshape(k, n // DIL_HEAD_DIM, DIL_HEAD_DIM)
    wh = jnp.concatenate([wh[:, :, :half], wh[:, :, ROT_DIMS:ROT_DIMS + gap],
                          wh[:, :, half:ROT_DIMS], wh[:, :, ROT_DIMS + gap:]], axis=2)
    return wh.reshape(k, n)


def kernel(x, ret_w_in, ret_w_out, kv_w, dil_w_q, dil_w_out, mlp_w_up, mlp_w_down, ln_g, ln_b):
    batch, seq, _ = x.shape
    xf = x.reshape(batch * seq, D_MODEL)

    cos_r, sin_r, decay, xi_b, zeta_b, gc_b = _retention_tables(seq)
    dil_tm = 512
    rope_tabs = _rope_tables(seq)
    attn_tq = (1024, 512, 256)
    attn_rps = (1, 2, 2)

    n_qk = 2 * D_MODEL
    n_k = N_GROUPS * D_MODEL
    q_scale = DIL_HEAD_DIM ** -0.5
    k_units = ((True, 1.0),) * N_GROUPS
    v_units = ((False, 1.0),) * N_GROUPS
    q_units = ((True, q_scale),) * N_GROUPS

    w_in = ret_w_in.astype(BF16)
    w_in = jnp.concatenate([_pair_split_cols(w_in[:, :, :n_qk], RET_QK_DIM), w_in[:, :, n_qk:]], axis=2)
    kv_w_b = kv_w.astype(BF16)
    dil_w_q_b = dil_w_q.astype(BF16)
    ret_w_out_b = ret_w_out.astype(BF16)
    dil_w_out_b = dil_w_out.astype(BF16)
    w_up = mlp_w_up.astype(BF16)
    w_down = mlp_w_down.astype(BF16)
    ln = jnp.stack([ln_g[:, 0], ln_b[:, 0], ln_g[:, 1], ln_b[:, 1]], axis=1)

    ks = vs = qs = None
    for l in range(DEPTH):
        if l < N_A:
            proj = _ret_project(xf, w_in, l, cos_r, sin_r, seq=seq)
            mix = _retention(proj, decay, xi_b, zeta_b, gc_b, batch=batch, seq=seq)
            xf = _layer_tail([mix], [], xf, ret_w_out_b, l, ln[l], w_up, w_down, l)
        else:
            if l > N_A:
                w_q = _rope_cols(dil_w_q_b[l - N_A])
                qs = _dil_project(xf, w_q, rope_tabs, q_units, batch=batch, seq=seq, tm=2 * dil_tm)
            outs, lses = [], []
            for gi, d in enumerate(DILATIONS):
                o, lse = _dilated_attention(qs[gi], ks[gi], vs[gi], d, batch=batch, seq=seq,
                                            tq=attn_tq[gi], rps=attn_rps[gi])
                outs.append(o)
                lses.append(lse)
            xf = _layer_tail(outs, lses, xf, dil_w_out_b, l - N_A, ln[l], w_up, w_down, l)
        if l == N_A - 1:
            w_kvq = jnp.concatenate([_rope_cols(kv_w_b[:, :n_k]), kv_w_b[:, n_k:],
                                     _rope_cols(dil_w_q_b[0])], axis=1)
            kvq = _dil_project(xf, w_kvq, rope_tabs, k_units + v_units + q_units,
                               batch=batch, seq=seq, tm=dil_tm)
            ks, vs, qs = kvq[:N_GROUPS], kvq[N_GROUPS:2 * N_GROUPS], kvq[2 * N_GROUPS:]
    return xf.reshape(batch, seq, D_MODEL)
```

```python
import functools

import jax
import jax.numpy as jnp
from jax import lax
from jax.experimental import pallas as pl
from jax.experimental.pallas import tpu as pltpu

D_MODEL = 1024
DEPTH = 4
N_A = DEPTH // 2

RET_HEADS = 4
RET_QK_DIM = D_MODEL // RET_HEADS
RET_V_DIM = 2 * RET_QK_DIM
RET_CHUNK = 256
RET_THETA = 10000.0
RET_VW = RET_HEADS * RET_V_DIM

DIL_GROUPS = ((128, 1), (512, 4), (2048, 16))
DILATIONS = tuple(d for _, d in DIL_GROUPS)
N_GROUPS = len(DIL_GROUPS)
DIL_HEADS = 8
DIL_HEAD_DIM = D_MODEL // DIL_HEADS
ROT_DIMS = DIL_HEAD_DIM // 4
ROPE_THETA = 500000.0
DIL_BLK = 128

D_FF = 4 * D_MODEL
ALPHA = (2.0 * DEPTH) ** 0.25
LN_EPS = 1e-5
GN_EPS = 1e-6

LANES = 128
N_LANE_BLOCKS = D_MODEL // LANES
VMEM_LIMIT = 56 * 1024 * 1024
MASKED = -1e30

BF16 = jnp.bfloat16
F32 = jnp.float32


def _params(*sem):
    return pltpu.CompilerParams(dimension_semantics=sem, vmem_limit_bytes=VMEM_LIMIT)


def _resident(shape):
    return pl.BlockSpec(shape, lambda *_: (0,) * len(shape), pipeline_mode=pl.Buffered(1))


def _resident_layer(shape, layer):
    return pl.BlockSpec((None,) + tuple(shape), lambda *_: (layer,) + (0,) * len(shape),
                        pipeline_mode=pl.Buffered(1))


def _layernorm(y, g, b):
    mu = jnp.mean(y, axis=-1, keepdims=True)
    yc = y - mu
    var = jnp.mean(yc * yc, axis=-1, keepdims=True)
    return yc * lax.rsqrt(var + LN_EPS) * g + b


def _ret_proj_kernel(x_ref, w_ref, c_ref, s_ref, o_ref, *, tn):
    xb = x_ref[...].astype(BF16)
    c_q, s_q = c_ref[...], s_ref[...]
    k_scale = RET_QK_DIM ** -0.5
    c_k, s_k = c_q * k_scale, s_q * k_scale
    for j in range(w_ref.shape[1] // tn):
        at = j * tn
        acc = jnp.dot(xb, w_ref[:, at:at + tn], preferred_element_type=F32)
        if at >= 2 * D_MODEL:
            o_ref[:, at:at + tn] = acc.astype(BF16)
            continue
        c, s = (c_q, s_q) if at < D_MODEL else (c_k, s_k)
        for h in range(tn // (2 * LANES)):
            ev = 2 * h * LANES
            od = ev + LANES
            e, o = acc[:, ev:ev + LANES], acc[:, od:od + LANES]
            o_ref[:, at + ev:at + ev + LANES] = (e * c - o * s).astype(BF16)
            o_ref[:, at + od:at + od + LANES] = (o * c + e * s).astype(BF16)


def _ret_project(x, w_stack, layer, cos, sin, *, seq, tm=1024, tn=1024):
    m, kdim = x.shape
    n = w_stack.shape[2]
    nseq = seq // tm
    return pl.pallas_call(
        functools.partial(_ret_proj_kernel, tn=tn),
        out_shape=jax.ShapeDtypeStruct((m, n), BF16),
        grid=(m // tm,),
        in_specs=[
            pl.BlockSpec((tm, kdim), lambda i: (i, 0)),
            _resident_layer((kdim, n), layer),
            pl.BlockSpec((tm, LANES), lambda i: (i % nseq, 0)),
            pl.BlockSpec((tm, LANES), lambda i: (i % nseq, 0)),
        ],
        out_specs=pl.BlockSpec((tm, n), lambda i: (i, 0)),
        compiler_params=_params("parallel"),
        name="ret_proj",
    )(x, w_stack, cos, sin)


def _ret_kernel(q_ref, k_ref, v_ref, g_ref, dec_ref, xi_ref, zeta_ref, gc_ref, o_ref,
                r_ref, s_scr, kv_scr, rb_scr, o_scr):
    @pl.when(pl.program_id(2) == 0)
    def _():
        r_ref[...] = jnp.zeros_like(r_ref)

    c_sz = RET_CHUNK
    n_chunks = q_ref.shape[0] // c_sz
    for h in range(r_ref.shape[0]):
        qk = slice(h * RET_QK_DIM, (h + 1) * RET_QK_DIM)
        vg = slice(h * RET_V_DIM, (h + 1) * RET_V_DIM)
        for c in range(n_chunks):
            rows = slice(c * c_sz, (c + 1) * c_sz)
            s = lax.dot_general(q_ref[rows, qk], k_ref[rows, qk], (((1,), (1,)), ((), ())),
                                preferred_element_type=F32)
            s_scr[c] = (s * dec_ref[h]).astype(BF16)
            kz = (k_ref[rows, qk].astype(F32) * zeta_ref[h]).astype(BF16)
            kv_scr[c] = lax.dot_general(kz, v_ref[rows, vg], (((0,), (0,)), ((), ())),
                                        preferred_element_type=F32)
        r = r_ref[h]
        for c in range(n_chunks):
            rb_scr[c] = r.astype(BF16)
            r = gc_ref[h] * r + kv_scr[c]
        r_ref[h] = r
        for c in range(n_chunks):
            rows = slice(c * c_sz, (c + 1) * c_sz)
            o = jnp.dot(s_scr[c], v_ref[rows, vg], preferred_element_type=F32)
            o_scr[rows, :] = o + jnp.dot(q_ref[rows, qk], rb_scr[c], preferred_element_type=F32) * xi_ref[h]
        o = o_scr[...]
        of = o * lax.rsqrt(jnp.mean(o * o, axis=-1, keepdims=True) + GN_EPS)
        gate = g_ref[:, vg].astype(F32)
        o_ref[:, vg] = (gate * jax.nn.sigmoid(gate) * of).astype(BF16)


def _retention(proj, dec, xi, zeta, gc, *, batch, seq, tc=2048, hps=1):
    ns = seq // tc
    n_chunks = tc // RET_CHUNK
    qb = hps * RET_QK_DIM
    vb = hps * RET_V_DIM
    row = lambda b, i: b * ns + i
    per_head = lambda width: pl.BlockSpec((hps, RET_CHUNK, width), lambda b, h, i: (h, 0, 0))
    return pl.pallas_call(
        _ret_kernel,
        out_shape=jax.ShapeDtypeStruct((batch * seq, RET_VW), BF16),
        grid=(batch, RET_HEADS // hps, ns),
        in_specs=[
            pl.BlockSpec((tc, qb), lambda b, h, i: (row(b, i), h)),
            pl.BlockSpec((tc, qb), lambda b, h, i: (row(b, i), D_MODEL // qb + h)),
            pl.BlockSpec((tc, vb), lambda b, h, i: (row(b, i), 2 * D_MODEL // vb + h)),
            pl.BlockSpec((tc, vb), lambda b, h, i: (row(b, i), (2 * D_MODEL + RET_VW) // vb + h)),
            per_head(RET_CHUNK), per_head(RET_V_DIM), per_head(RET_QK_DIM),
            pl.BlockSpec((hps, 1, RET_V_DIM), lambda b, h, i: (h, 0, 0)),
        ],
        out_specs=pl.BlockSpec((tc, vb), lambda b, h, i: (row(b, i), h)),
        scratch_shapes=[pltpu.VMEM((hps, RET_QK_DIM, RET_V_DIM), F32),
                        pltpu.VMEM((n_chunks, RET_CHUNK, RET_CHUNK), BF16),
                        pltpu.VMEM((n_chunks, RET_QK_DIM, RET_V_DIM), F32),
                        pltpu.VMEM((n_chunks, RET_QK_DIM, RET_V_DIM), BF16),
                        pltpu.VMEM((tc, RET_V_DIM), F32)],
        compiler_params=_params("parallel", "parallel", "arbitrary"),
        name="retention",
    )(proj, proj, proj, proj, dec, xi, zeta, gc)


def _finish_layer(mix, x, w_out_ref, ln_ref, wu_ref, wd_ref, o_ref, *, ff_chunk):
    x1 = _layernorm(ALPHA * x + jnp.dot(mix, w_out_ref[...], preferred_element_type=F32),
                    ln_ref[0:1, :], ln_ref[1:2, :])
    xb = x1.astype(BF16)
    y = ALPHA * x1
    for c in range(wu_ref.shape[1] // ff_chunk):
        cols = slice(c * ff_chunk, (c + 1) * ff_chunk)
        h = jnp.maximum(jnp.dot(xb, wu_ref[:, cols], preferred_element_type=F32), 0.0)
        y = y + jnp.dot((h * h).astype(BF16), wd_ref[cols, :], preferred_element_type=F32)
    o_ref[...] = _layernorm(y, ln_ref[2:3, :], ln_ref[3:4, :])


def _ret_tail_kernel(a_ref, x_ref, w_out_ref, ln_ref, wu_ref, wd_ref, o_ref, *, ff_chunk):
    _finish_layer(a_ref[...], x_ref[...], w_out_ref, ln_ref, wu_ref, wd_ref, o_ref, ff_chunk=ff_chunk)


def _dil_tail_kernel(o1_ref, o2_ref, o3_ref, l1_ref, l2_ref, l3_ref, x_ref, w_out_ref, ln_ref, wu_ref, wd_ref,
                     o_ref, mix_a, mix_b, *, ff_chunk):
    s = pl.program_id(0)

    @pl.when(s == 0)
    def _():
        mix_b[...] = jnp.zeros_like(mix_b)

    def step(cur, prev):
        l1, l2, l3 = l1_ref[...], l2_ref[...], l3_ref[...]
        m = jnp.maximum(jnp.maximum(l1, l2), l3)
        e1, e2, e3 = jnp.exp(l1 - m), jnp.exp(l2 - m), jnp.exp(l3 - m)
        inv = 1.0 / (e1 + e2 + e3)
        w1, w2 = e1 * inv, e2 * inv
        for h in range(DIL_HEADS):
            cols = slice(h * DIL_HEAD_DIM, (h + 1) * DIL_HEAD_DIM)
            o3 = o3_ref[:, cols].astype(F32)
            comb = (o3 + w1[:, h:h + 1] * (o1_ref[:, cols].astype(F32) - o3)
                    + w2[:, h:h + 1] * (o2_ref[:, cols].astype(F32) - o3))
            cur[:, cols] = comb.astype(BF16)
        _finish_layer(prev[...], x_ref[...], w_out_ref, ln_ref, wu_ref, wd_ref, o_ref, ff_chunk=ff_chunk)

    @pl.when(s % 2 == 0)
    def _():
        step(mix_a, mix_b)

    @pl.when(s % 2 == 1)
    def _():
        step(mix_b, mix_a)


def _layer_tail(mixes, lses, x, w_out_stack, mixer_layer, ln, wu_stack, wd_stack, layer, *, tm=512, ff_chunk=1024):
    m = x.shape[0]
    n = m // tm
    dilated = len(mixes) > 1
    kdim = w_out_stack.shape[1]
    if dilated:
        ahead = lambda width: pl.BlockSpec((tm, width), lambda s: (jnp.minimum(s, n - 1), 0))
        tile = lambda width: pl.BlockSpec((tm, width), lambda s: (jnp.maximum(s - 1, 0), 0))
    else:
        ahead = tile = lambda width: pl.BlockSpec((tm, width), lambda i: (i, 0))
    in_specs = ([ahead(a.shape[1]) for a in mixes] + [ahead(LANES)] * len(lses) + [
        tile(D_MODEL), _resident_layer((kdim, D_MODEL), mixer_layer), _resident((4, D_MODEL)),
        _resident_layer((D_MODEL, D_FF), layer), _resident_layer((D_FF, D_MODEL), layer)])
    kern = _dil_tail_kernel if dilated else _ret_tail_kernel
    return pl.pallas_call(
        functools.partial(kern, ff_chunk=ff_chunk),
        out_shape=jax.ShapeDtypeStruct((m, D_MODEL), F32),
        grid=(n + 1 if dilated else n,),
        in_specs=in_specs,
        out_specs=tile(D_MODEL),
        scratch_shapes=[pltpu.VMEM((tm, D_MODEL), BF16)] * 2 if dilated else [],
        compiler_params=_params("arbitrary"),
        name="dil_tail" if dilated else "ret_tail",
    )(*mixes, *lses, x, w_out_stack, ln, wu_stack, wd_stack)


def _dil_proj_kernel(*refs, units, tm):
    nb = N_LANE_BLOCKS
    x_refs = refs[:nb]
    w_ref, cos_ref, sin_ref = refs[nb:nb + 3]
    outs = refs[nb + 3:-3]
    xb_ref, tab_ref, tmp_ref = refs[-3:]

    def put(gi, src, rows, val):
        if src < nb:
            xb_ref[gi, rows, src * LANES:(src + 1) * LANES] = val.astype(BF16)
        else:
            tab_ref[gi, src - nb, rows, :] = val

    step = DILATIONS[1]
    assert DILATIONS == (1, step, step * step)
    n1, n2 = tm // step, tm // (step * step)
    for src, ref in enumerate(list(x_refs) + [cos_ref, sin_ref]):
        put(0, src, slice(0, tm), ref[...])
        for a in range(step):
            t = ref[pl.ds(a, n1, stride=step), :]
            put(1, src, slice(a * n1, (a + 1) * n1), t)
            tmp_ref[src, a] = t
        for a in range(step):
            for b in range(step):
                r = step * b + a
                put(2, src, slice(r * n2, (r + 1) * n2), tmp_ref[src, a, pl.ds(b, n2, stride=step), :])

    for jj, (rope, scale) in enumerate(units):
        gi = jj % N_GROUPS
        d = DILATIONS[gi]
        n = tm // d
        acc = jnp.dot(xb_ref[gi], w_ref[:, jj * D_MODEL:(jj + 1) * D_MODEL], preferred_element_type=F32)
        for h in range(DIL_HEADS):
            t = acc[:, h * LANES:(h + 1) * LANES]
            if rope:
                t = t * tab_ref[gi, 0] + pltpu.roll(t, LANES // 2, axis=1) * tab_ref[gi, 1]
            if scale != 1.0:
                t = t * scale
            tb = t.astype(BF16)
            for r in range(d):
                outs[jj][:, r * D_MODEL + h * LANES:r * D_MODEL + (h + 1) * LANES] = tb[r * n:(r + 1) * n]


def _dil_project(x, w, tables, units, *, batch, seq, tm=512):
    nseq = seq // tm
    out_shape, out_specs = [], []
    for jj in range(len(units)):
        d = DILATIONS[jj % N_GROUPS]
        out_shape.append(jax.ShapeDtypeStruct((batch, seq // d, d * D_MODEL), BF16))
        out_specs.append(pl.BlockSpec((None, tm // d, d * D_MODEL), lambda i: (i // nseq, i % nseq, 0)))
    x_specs = [pl.BlockSpec((tm, LANES), functools.partial(lambda i, c: (i, c), c=c))
               for c in range(N_LANE_BLOCKS)]
    tab_spec = pl.BlockSpec((tm, LANES), lambda i: (i % nseq, 0))
    return pl.pallas_call(
        functools.partial(_dil_proj_kernel, units=units, tm=tm),
        out_shape=out_shape,
        grid=(x.shape[0] // tm,),
        in_specs=x_specs + [_resident(w.shape), tab_spec, tab_spec],
        out_specs=out_specs,
        scratch_shapes=[pltpu.VMEM((N_GROUPS, tm, D_MODEL), BF16),
                        pltpu.VMEM((N_GROUPS, 2, tm, LANES), F32),
                        pltpu.VMEM((N_LANE_BLOCKS + 2, DILATIONS[1], tm // DILATIONS[1], LANES), F32)],
        compiler_params=_params("parallel"),
        name="dil_proj",
    )(*([x] * N_LANE_BLOCKS), w, *tables)


def _dil_attn_kernel(q_ref, kc_ref, vc_ref, kp_ref, vp_ref, o_ref, lse_ref,
                     s_scr, p_scr, kt_scr, v_scr, *acc, dilation, rps):
    blk = DIL_BLK
    d = dilation
    pid = pl.program_id(2)
    no_prev = (pl.program_id(1) == 0).astype(jnp.int32)
    row2 = lax.broadcasted_iota(jnp.int32, (blk, 2 * blk), 0)
    col2 = lax.broadcasted_iota(jnp.int32, (blk, 2 * blk), 1)
    band_ok = jnp.logical_and(col2 >= row2, col2 <= row2 + blk)
    first_band_ok = jnp.logical_and(col2 >= row2 * (1 - no_prev) + no_prev * blk, col2 <= row2 + blk)
    lane = lax.broadcasted_iota(jnp.int32, (blk, LANES), 1)
    n_blk = q_ref.shape[0] // blk
    hd = DIL_HEAD_DIM
    ones = jnp.ones((q_ref.shape[0] + blk, hd), BF16)
    for rr in range(rps):
        base = rr * D_MODEL
        r = pid * rps + rr
        for h in range(DIL_HEADS):
            cols = slice(base + h * hd, base + (h + 1) * hd)
            v_scr[rr, :blk, 2 * h * hd:(2 * h + 1) * hd] = vp_ref[:, cols]
            v_scr[rr, blk:, 2 * h * hd:(2 * h + 1) * hd] = vc_ref[:, cols]
            v_scr[rr, :, (2 * h + 1) * hd:(2 * h + 2) * hd] = ones
            kt_scr[rr, h, :, :blk] = kp_ref[:, cols].T
            for c in range(n_blk):
                kt_scr[rr, h, :, (c + 1) * blk:(c + 2) * blk] = kc_ref[c * blk:(c + 1) * blk, cols].T
        for c in range(n_blk):
            rows = slice(c * blk, (c + 1) * blk)
            win = slice(c * blk, (c + 2) * blk)
            tok = rows if d == 1 else pl.ds(c * blk * d + r, blk, stride=d)
            for h in range(DIL_HEADS):
                cols = slice(base + h * hd, base + (h + 1) * hd)
                s = jnp.dot(q_ref[rows, cols], kt_scr[rr, h, :, win], preferred_element_type=F32)
                s_scr[rr, c, h] = jnp.where(band_ok if c else first_band_ok, s, MASKED)
            s_all = s_scr[rr, c]
            m = jnp.max(s_all, axis=-1, keepdims=True)
            p_scr[rr, c] = jnp.exp(s_all - m).astype(BF16)
            lse_tile = jnp.zeros((blk, LANES), F32)
            for h in range(DIL_HEADS):
                ol = jnp.dot(p_scr[rr, c, h], v_scr[rr, win, 2 * h * hd:(2 * h + 2) * hd],
                             preferred_element_type=F32)
                l = ol[:, hd:]
                o = ol[:, :hd] * (1.0 / l)
                if d == 1:
                    o_ref[rows, h * hd:(h + 1) * hd] = o.astype(BF16)
                else:
                    acc[0][h, tok, :] = o
                lse_tile = jnp.where(lane == h, m[h] + jnp.log(l), lse_tile)
            lse_ref[tok, :] = lse_tile

    if d > 1:
        @pl.when(pid == d // rps - 1)
        def _():
            for h in range(DIL_HEADS):
                o_ref[:, h * DIL_HEAD_DIM:(h + 1) * DIL_HEAD_DIM] = acc[0][h].astype(BF16)


def _dilated_attention(q, k, v, dilation, *, batch, seq, tq, rps):
    d = dilation
    n_sub = seq // d
    nq = n_sub // tq
    per = tq // DIL_BLK
    rows_out = tq * d
    width = rps * D_MODEL
    cur = pl.BlockSpec((None, tq, width), lambda b, i, r: (b, i, r))
    prev = pl.BlockSpec((None, DIL_BLK, width), lambda b, i, r: (b, jnp.maximum(i * per - 1, 0), r))
    scratch = [pltpu.VMEM((rps, per, DIL_HEADS, DIL_BLK, 2 * DIL_BLK), F32),
               pltpu.VMEM((rps, per, DIL_HEADS, DIL_BLK, 2 * DIL_BLK), BF16),
               pltpu.VMEM((rps, DIL_HEADS, DIL_HEAD_DIM, tq + DIL_BLK), BF16),
               pltpu.VMEM((rps, tq + DIL_BLK, 2 * D_MODEL), BF16)]
    if d > 1:
        scratch.append(pltpu.VMEM((DIL_HEADS, rows_out, DIL_HEAD_DIM), F32))
    return pl.pallas_call(
        functools.partial(_dil_attn_kernel, dilation=d, rps=rps),
        out_shape=(jax.ShapeDtypeStruct((batch * seq, D_MODEL), BF16),
                   jax.ShapeDtypeStruct((batch * seq, LANES), F32)),
        grid=(batch, nq, d // rps),
        in_specs=[cur, cur, cur, prev, prev],
        out_specs=(pl.BlockSpec((rows_out, D_MODEL), lambda b, i, r: (b * nq + i, 0)),
                   pl.BlockSpec((rows_out, LANES), lambda b, i, r: (b * nq + i, 0))),
        scratch_shapes=scratch,
        compiler_params=_params("parallel", "arbitrary", "arbitrary"),
        name=f"dil_attn_d{d}",
    )(q, k, v, k, v)


def _retention_tables(seq):
    half = RET_QK_DIM // 2
    angle = 1.0 / (RET_THETA ** jnp.linspace(0.0, 1.0, half, dtype=F32))
    ang = jnp.arange(seq, dtype=F32)[:, None] * angle[None]
    c_sz = RET_CHUNK
    log_g = jnp.log(1.0 - 2.0 ** (-5.0 - jnp.arange(RET_HEADS, dtype=F32)))
    idx = jnp.arange(c_sz, dtype=F32)
    diff = idx[:, None] - idx[None, :]
    decay = jnp.where(diff[None] >= 0, jnp.exp(jnp.maximum(diff, 0.0)[None] * log_g[:, None, None]), 0.0)
    xi = jnp.exp((idx[None] + 1.0) * log_g[:, None])
    zeta = jnp.exp((c_sz - 1.0 - idx[None]) * log_g[:, None])
    g_chunk = jnp.exp(c_sz * log_g)
    xi_b = jnp.broadcast_to(xi[:, :, None], (RET_HEADS, c_sz, RET_V_DIM))
    zeta_b = jnp.broadcast_to(zeta[:, :, None], (RET_HEADS, c_sz, RET_QK_DIM))
    gc_b = jnp.broadcast_to(g_chunk[:, None, None], (RET_HEADS, 1, RET_V_DIM))
    return jnp.cos(ang), jnp.sin(ang), decay, xi_b, zeta_b, gc_b


def _rope_tables(seq):
    half = ROT_DIMS // 2
    inv_freq = ROPE_THETA ** (-jnp.arange(0, ROT_DIMS, 2, dtype=F32) / ROT_DIMS)
    ang = jnp.arange(seq, dtype=F32)[:, None] * inv_freq[None]
    cos, sin = jnp.cos(ang), jnp.sin(ang)
    gap = LANES // 2 - half
    ones = jnp.ones((seq, gap), F32)
    zeros = jnp.zeros((seq, gap), F32)
    return (jnp.concatenate([cos, ones, cos, ones], axis=1),
            jnp.concatenate([-sin, zeros, sin, zeros], axis=1))


def _pair_split_cols(w, dim):
    lead, n = w.shape[:-1], w.shape[-1]
    return w.reshape(*lead, n // dim, dim // 2, 2).swapaxes(-1, -2).reshape(*lead, n)


def _rope_cols(w):
    k, n = w.shape
    half = ROT_DIMS // 2
    gap = LANES // 2 - half
    wh = w.reshape(k, n // DIL_HEAD_DIM, DIL_HEAD_DIM)
    wh = jnp.concatenate([wh[:, :, :half], wh[:, :, ROT_DIMS:ROT_DIMS + gap],
                          wh[:, :, half:ROT_DIMS], wh[:, :, ROT_DIMS + gap:]], axis=2)
    return wh.reshape(k, n)


def kernel(x, ret_w_in, ret_w_out, kv_w, dil_w_q, dil_w_out, mlp_w_up, mlp_w_down, ln_g, ln_b):
    batch, seq, _ = x.shape
    xf = x.reshape(batch * seq, D_MODEL)

    cos_r, sin_r, decay, xi_b, zeta_b, gc_b = _retention_tables(seq)
    dil_tm = 512
    rope_tabs = _rope_tables(seq)
    attn_tq = (1024, 512, 256)
    attn_rps = (1, 2, 2)

    n_qk = 2 * D_MODEL
    n_k = N_GROUPS * D_MODEL
    q_scale = DIL_HEAD_DIM ** -0.5
    k_units = ((True, 1.0),) * N_GROUPS
    v_units = ((False, 1.0),) * N_GROUPS
    q_units = ((True, q_scale),) * N_GROUPS

    w_in = ret_w_in.astype(BF16)
    w_in = jnp.concatenate([_pair_split_cols(w_in[:, :, :n_qk], RET_QK_DIM), w_in[:, :, n_qk:]], axis=2)
    kv_w_b = kv_w.astype(BF16)
    dil_w_q_b = dil_w_q.astype(BF16)
    ret_w_out_b = ret_w_out.astype(BF16)
    dil_w_out_b = dil_w_out.astype(BF16)
    w_up = mlp_w_up.astype(BF16)
    w_down = mlp_w_down.astype(BF16)
    ln = jnp.stack([ln_g[:, 0], ln_b[:, 0], ln_g[:, 1], ln_b[:, 1]], axis=1)

    ks = vs = qs = None
    for l in range(DEPTH):
        if l < N_A:
            proj = _ret_project(xf, w_in, l, cos_r, sin_r, seq=seq)
            mix = _retention(proj, decay, xi_b, zeta_b, gc_b, batch=batch, seq=seq)
            xf = _layer_tail([mix], [], xf, ret_w_out_b, l, ln[l], w_up, w_down, l)
        else:
            if l > N_A:
                w_q = _rope_cols(dil_w_q_b[l - N_A])
                qs = _dil_project(xf, w_q, rope_tabs, q_units, batch=batch, seq=seq, tm=2 * dil_tm)
            outs, lses = [], []
            for gi, d in enumerate(DILATIONS):
                o, lse = _dilated_attention(qs[gi], ks[gi], vs[gi], d, batch=batch, seq=seq,
                                            tq=attn_tq[gi], rps=attn_rps[gi])
                outs.append(o)
                lses.append(lse)
            xf = _layer_tail(outs, lses, xf, dil_w_out_b, l - N_A, ln[l], w_up, w_down, l)
        if l == N_A - 1:
            w_kvq = jnp.concatenate([_rope_cols(kv_w_b[:, :n_k]), kv_w_b[:, n_k:],
                                     _rope_cols(dil_w_q_b[0])], axis=1)
            kvq = _dil_project(xf, w_kvq, rope_tabs, k_units + v_units + q_units,
                               batch=batch, seq=seq, tm=dil_tm)
            ks, vs, qs = kvq[:N_GROUPS], kvq[N_GROUPS:2 * N_GROUPS], kvq[2 * N_GROUPS:]
    return xf.reshape(batch, seq, D_MODEL)
```

```python
import functools

import jax
import jax.numpy as jnp
from jax import lax
from jax.experimental import pallas as pl
from jax.experimental.pallas import tpu as pltpu

D_MODEL = 1024
DEPTH = 4
N_A = DEPTH // 2

RET_HEADS = 4
RET_QK_DIM = D_MODEL // RET_HEADS
RET_V_DIM = 2 * RET_QK_DIM
RET_CHUNK = 256
RET_THETA = 10000.0
RET_VW = RET_HEADS * RET_V_DIM

DIL_GROUPS = ((128, 1), (512, 4), (2048, 16))
DILATIONS = tuple(d for _, d in DIL_GROUPS)
N_GROUPS = len(DIL_GROUPS)
DIL_HEADS = 8
DIL_HEAD_DIM = D_MODEL // DIL_HEADS
ROT_DIMS = DIL_HEAD_DIM // 4
ROPE_THETA = 500000.0
DIL_BLK = 128

D_FF = 4 * D_MODEL
ALPHA = (2.0 * DEPTH) ** 0.25
LN_EPS = 1e-5
GN_EPS = 1e-6

LANES = 128
N_LANE_BLOCKS = D_MODEL // LANES
VMEM_LIMIT = 56 * 1024 * 1024
MASKED = -1e30

BF16 = jnp.bfloat16
F32 = jnp.float32


def _params(*sem):
    return pltpu.CompilerParams(dimension_semantics=sem, vmem_limit_bytes=VMEM_LIMIT)


def _resident(shape):
    return pl.BlockSpec(shape, lambda *_: (0,) * len(shape), pipeline_mode=pl.Buffered(1))


def _resident_layer(shape, layer):
    return pl.BlockSpec((None,) + tuple(shape), lambda *_: (layer,) + (0,) * len(shape),
                        pipeline_mode=pl.Buffered(1))


def _layernorm(y, g, b):
    mu = jnp.mean(y, axis=-1, keepdims=True)
    yc = y - mu
    var = jnp.mean(yc * yc, axis=-1, keepdims=True)
    return yc * lax.rsqrt(var + LN_EPS) * g + b


def _ret_proj_kernel(x_ref, w_ref, c_ref, s_ref, o_ref, *, tn):
    xb = x_ref[...].astype(BF16)
    c_q, s_q = c_ref[...], s_ref[...]
    k_scale = RET_QK_DIM ** -0.5
    c_k, s_k = c_q * k_scale, s_q * k_scale
    for j in range(w_ref.shape[1] // tn):
        at = j * tn
        acc = jnp.dot(xb, w_ref[:, at:at + tn], preferred_element_type=F32)
        if at >= 2 * D_MODEL:
            o_ref[:, at:at + tn] = acc.astype(BF16)
            continue
        c, s = (c_q, s_q) if at < D_MODEL else (c_k, s_k)
        for h in range(tn // (2 * LANES)):
            ev = 2 * h * LANES
            od = ev + LANES
            e, o = acc[:, ev:ev + LANES], acc[:, od:od + LANES]
            o_ref[:, at + ev:at + ev + LANES] = (e * c - o * s).astype(BF16)
            o_ref[:, at + od:at + od + LANES] = (o * c + e * s).astype(BF16)


def _ret_project(x, w_stack, layer, cos, sin, *, seq, tm=1024, tn=1024):
    m, kdim = x.shape
    n = w_stack.shape[2]
    nseq = seq // tm
    return pl.pallas_call(
        functools.partial(_ret_proj_kernel, tn=tn),
        out_shape=jax.ShapeDtypeStruct((m, n), BF16),
        grid=(m // tm,),
        in_specs=[
            pl.BlockSpec((tm, kdim), lambda i: (i, 0)),
            _resident_layer((kdim, n), layer),
            pl.BlockSpec((tm, LANES), lambda i: (i % nseq, 0)),
            pl.BlockSpec((tm, LANES), lambda i: (i % nseq, 0)),
        ],
        out_specs=pl.BlockSpec((tm, n), lambda i: (i, 0)),
        compiler_params=_params("parallel"),
        name="ret_proj",
    )(x, w_stack, cos, sin)


def _ret_kernel(q_ref, k_ref, v_ref, g_ref, dec_ref, xi_ref, zeta_ref, gc_ref, o_ref,
                r_ref, s_scr, kv_scr, rb_scr, o_scr):
    @pl.when(pl.program_id(2) == 0)
    def _():
        r_ref[...] = jnp.zeros_like(r_ref)

    c_sz = RET_CHUNK
    n_chunks = q_ref.shape[0] // c_sz
    for h in range(r_ref.shape[0]):
        qk = slice(h * RET_QK_DIM, (h + 1) * RET_QK_DIM)
        vg = slice(h * RET_V_DIM, (h + 1) * RET_V_DIM)
        for c in range(n_chunks):
            rows = slice(c * c_sz, (c + 1) * c_sz)
            s = lax.dot_general(q_ref[rows, qk], k_ref[rows, qk], (((1,), (1,)), ((), ())),
                                preferred_element_type=F32)
            s_scr[c] = (s * dec_ref[h]).astype(BF16)
            kz = (k_ref[rows, qk].astype(F32) * zeta_ref[h]).astype(BF16)
            kv_scr[c] = lax.dot_general(kz, v_ref[rows, vg], (((0,), (0,)), ((), ())),
                                        preferred_element_type=F32)
        r = r_ref[h]
        for c in range(n_chunks):
            rb_scr[c] = r.astype(BF16)
            r = gc_ref[h] * r + kv_scr[c]
        r_ref[h] = r
        for c in range(n_chunks):
            rows = slice(c * c_sz, (c + 1) * c_sz)
            o = jnp.dot(s_scr[c], v_ref[rows, vg], preferred_element_type=F32)
            o_scr[rows, :] = o + jnp.dot(q_ref[rows, qk], rb_scr[c], preferred_element_type=F32) * xi_ref[h]
        o = o_scr[...]
        of = o * lax.rsqrt(jnp.mean(o * o, axis=-1, keepdims=True) + GN_EPS)
        gate = g_ref[:, vg].astype(F32)
        o_ref[:, vg] = (gate * jax.nn.sigmoid(gate) * of).astype(BF16)


def _retention(proj, dec, xi, zeta, gc, *, batch, seq, tc=2048, hps=1):
    ns = seq // tc
    n_chunks = tc // RET_CHUNK
    qb = hps * RET_QK_DIM
    vb = hps * RET_V_DIM
    row = lambda b, i: b * ns + i
    per_head = lambda width: pl.BlockSpec((hps, RET_CHUNK, width), lambda b, h, i: (h, 0, 0))
    return pl.pallas_call(
        _ret_kernel,
        out_shape=jax.ShapeDtypeStruct((batch * seq, RET_VW), BF16),
        grid=(batch, RET_HEADS // hps, ns),
        in_specs=[
            pl.BlockSpec((tc, qb), lambda b, h, i: (row(b, i), h)),
            pl.BlockSpec((tc, qb), lambda b, h, i: (row(b, i), D_MODEL // qb + h)),
            pl.BlockSpec((tc, vb), lambda b, h, i: (row(b, i), 2 * D_MODEL // vb + h)),
            pl.BlockSpec((tc, vb), lambda b, h, i: (row(b, i), (2 * D_MODEL + RET_VW) // vb + h)),
            per_head(RET_CHUNK), per_head(RET_V_DIM), per_head(RET_QK_DIM),
            pl.BlockSpec((hps, 1, RET_V_DIM), lambda b, h, i: (h, 0, 0)),
        ],
        out_specs=pl.BlockSpec((tc, vb), lambda b, h, i: (row(b, i), h)),
        scratch_shapes=[pltpu.VMEM((hps, RET_QK_DIM, RET_V_DIM), F32),
                        pltpu.VMEM((n_chunks, RET_CHUNK, RET_CHUNK), BF16),
                        pltpu.VMEM((n_chunks, RET_QK_DIM, RET_V_DIM), F32),
                        pltpu.VMEM((n_chunks, RET_QK_DIM, RET_V_DIM), BF16),
                        pltpu.VMEM((tc, RET_V_DIM), F32)],
        compiler_params=_params("parallel", "parallel", "arbitrary"),
        name="retention",
    )(proj, proj, proj, proj, dec, xi, zeta, gc)


def _finish_layer(mix, x, w_out_ref, ln_ref, wu_ref, wd_ref, o_ref, *, ff_chunk, rows=slice(None)):
    x1 = _layernorm(ALPHA * x + jnp.dot(mix, w_out_ref[...], preferred_element_type=F32),
                    ln_ref[0:1, :], ln_ref[1:2, :])
    xb = x1.astype(BF16)
    y = ALPHA * x1
    for c in range(wu_ref.shape[1] // ff_chunk):
        cols = slice(c * ff_chunk, (c + 1) * ff_chunk)
        h = jnp.maximum(jnp.dot(xb, wu_ref[:, cols], preferred_element_type=F32), 0.0)
        y = y + jnp.dot((h * h).astype(BF16), wd_ref[cols, :], preferred_element_type=F32)
    o_ref[rows, :] = _layernorm(y, ln_ref[2:3, :], ln_ref[3:4, :])


def _ret_tail_kernel(a_ref, x_ref, w_out_ref, ln_ref, wu_ref, wd_ref, o_ref, *, ff_chunk, sub):
    for p in range(a_ref.shape[0] // sub):
        rows = slice(p * sub, (p + 1) * sub)
        _finish_layer(a_ref[rows, :], x_ref[rows, :], w_out_ref, ln_ref, wu_ref, wd_ref, o_ref,
                      ff_chunk=ff_chunk, rows=rows)


def _dil_tail_kernel(o1_ref, o2_ref, o3_ref, l1_ref, l2_ref, l3_ref, x_ref, w_out_ref, ln_ref, wu_ref, wd_ref,
                     o_ref, mix_a, mix_b, *, ff_chunk):
    s = pl.program_id(0)

    @pl.when(s == 0)
    def _():
        mix_b[...] = jnp.zeros_like(mix_b)

    def step(cur, prev):
        l1, l2, l3 = l1_ref[...], l2_ref[...], l3_ref[...]
        m = jnp.maximum(jnp.maximum(l1, l2), l3)
        e1, e2, e3 = jnp.exp(l1 - m), jnp.exp(l2 - m), jnp.exp(l3 - m)
        inv = 1.0 / (e1 + e2 + e3)
        w1, w2 = e1 * inv, e2 * inv
        for h in range(DIL_HEADS):
            cols = slice(h * DIL_HEAD_DIM, (h + 1) * DIL_HEAD_DIM)
            o3 = o3_ref[:, cols].astype(F32)
            comb = (o3 + w1[:, h:h + 1] * (o1_ref[:, cols].astype(F32) - o3)
                    + w2[:, h:h + 1] * (o2_ref[:, cols].astype(F32) - o3))
            cur[:, cols] = comb.astype(BF16)
        _finish_layer(prev[...], x_ref[...], w_out_ref, ln_ref, wu_ref, wd_ref, o_ref, ff_chunk=ff_chunk)

    @pl.when(s % 2 == 0)
    def _():
        step(mix_a, mix_b)

    @pl.when(s % 2 == 1)
    def _():
        step(mix_b, mix_a)


def _layer_tail(mixes, lses, x, w_out_stack, mixer_layer, ln, wu_stack, wd_stack, layer, *, tm=512, ff_chunk=1024):
    m = x.shape[0]
    dilated = len(mixes) > 1
    sub = tm
    if not dilated:
        tm = 2 * tm
    n = m // tm
    kdim = w_out_stack.shape[1]
    if dilated:
        ahead = lambda width: pl.BlockSpec((tm, width), lambda s: (jnp.minimum(s, n - 1), 0))
        tile = lambda width: pl.BlockSpec((tm, width), lambda s: (jnp.maximum(s - 1, 0), 0))
    else:
        ahead = tile = lambda width: pl.BlockSpec((tm, width), lambda i: (i, 0))
    in_specs = ([ahead(a.shape[1]) for a in mixes] + [ahead(LANES)] * len(lses) + [
        tile(D_MODEL), _resident_layer((kdim, D_MODEL), mixer_layer), _resident((4, D_MODEL)),
        _resident_layer((D_MODEL, D_FF), layer), _resident_layer((D_FF, D_MODEL), layer)])
    kern = (functools.partial(_dil_tail_kernel, ff_chunk=ff_chunk) if dilated
            else functools.partial(_ret_tail_kernel, ff_chunk=ff_chunk, sub=sub))
    return pl.pallas_call(
        kern,
        out_shape=jax.ShapeDtypeStruct((m, D_MODEL), F32),
        grid=(n + 1 if dilated else n,),
        in_specs=in_specs,
        out_specs=tile(D_MODEL),
        scratch_shapes=[pltpu.VMEM((tm, D_MODEL), BF16)] * 2 if dilated else [],
        compiler_params=_params("arbitrary"),
        name="dil_tail" if dilated else "ret_tail",
    )(*mixes, *lses, x, w_out_stack, ln, wu_stack, wd_stack)


def _dil_proj_kernel(*refs, units, tm):
    nb = N_LANE_BLOCKS
    x_refs = refs[:nb]
    w_ref, cos_ref, sin_ref = refs[nb:nb + 3]
    outs = refs[nb + 3:-3]
    xb_ref, tab_ref, tmp_ref = refs[-3:]

    def put(gi, src, rows, val):
        if src < nb:
            xb_ref[gi, rows, src * LANES:(src + 1) * LANES] = val.astype(BF16)
        else:
            tab_ref[gi, src - nb, rows, :] = val

    step = DILATIONS[1]
    assert DILATIONS == (1, step, step * step)
    n1, n2 = tm // step, tm // (step * step)
    for src, ref in enumerate(list(x_refs) + [cos_ref, sin_ref]):
        put(0, src, slice(0, tm), ref[...])
        for a in range(step):
            t = ref[pl.ds(a, n1, stride=step), :]
            put(1, src, slice(a * n1, (a + 1) * n1), t)
            tmp_ref[src, a] = t
        for a in range(step):
            for b in range(step):
                r = step * b + a
                put(2, src, slice(r * n2, (r + 1) * n2), tmp_ref[src, a, pl.ds(b, n2, stride=step), :])

    for jj, (rope, scale) in enumerate(units):
        gi = jj % N_GROUPS
        d = DILATIONS[gi]
        n = tm // d
        acc = jnp.dot(xb_ref[gi], w_ref[:, jj * D_MODEL:(jj + 1) * D_MODEL], preferred_element_type=F32)
        for h in range(DIL_HEADS):
            t = acc[:, h * LANES:(h + 1) * LANES]
            if rope:
                t = t * tab_ref[gi, 0] + pltpu.roll(t, LANES // 2, axis=1) * tab_ref[gi, 1]
            if scale != 1.0:
                t = t * scale
            tb = t.astype(BF16)
            for r in range(d):
                outs[jj][:, r * D_MODEL + h * LANES:r * D_MODEL + (h + 1) * LANES] = tb[r * n:(r + 1) * n]


def _dil_project(x, w, tables, units, *, batch, seq, tm=512):
    nseq = seq // tm
    out_shape, out_specs = [], []
    for jj in range(len(units)):
        d = DILATIONS[jj % N_GROUPS]
        out_shape.append(jax.ShapeDtypeStruct((batch, seq // d, d * D_MODEL), BF16))
        out_specs.append(pl.BlockSpec((None, tm // d, d * D_MODEL), lambda i: (i // nseq, i % nseq, 0)))
    x_specs = [pl.BlockSpec((tm, LANES), functools.partial(lambda i, c: (i, c), c=c))
               for c in range(N_LANE_BLOCKS)]
    tab_spec = pl.BlockSpec((tm, LANES), lambda i: (i % nseq, 0))
    return pl.pallas_call(
        functools.partial(_dil_proj_kernel, units=units, tm=tm),
        out_shape=out_shape,
        grid=(x.shape[0] // tm,),
        in_specs=x_specs + [_resident(w.shape), tab_spec, tab_spec],
        out_specs=out_specs,
        scratch_shapes=[pltpu.VMEM((N_GROUPS, tm, D_MODEL), BF16),
                        pltpu.VMEM((N_GROUPS, 2, tm, LANES), F32),
                        pltpu.VMEM((N_LANE_BLOCKS + 2, DILATIONS[1], tm // DILATIONS[1], LANES), F32)],
        compiler_params=_params("parallel"),
        name="dil_proj",
    )(*([x] * N_LANE_BLOCKS), w, *tables)


def _dil_attn_kernel(q_ref, kc_ref, vc_ref, kp_ref, vp_ref, o_ref, lse_ref,
                     s_scr, p_scr, kt_scr, v_scr, *acc, dilation, rps):
    blk = DIL_BLK
    d = dilation
    pid = pl.program_id(2)
    no_prev = (pl.program_id(1) == 0).astype(jnp.int32)
    row2 = lax.broadcasted_iota(jnp.int32, (blk, 2 * blk), 0)
    col2 = lax.broadcasted_iota(jnp.int32, (blk, 2 * blk), 1)
    band_ok = jnp.logical_and(col2 >= row2, col2 <= row2 + blk)
    first_band_ok = jnp.logical_and(col2 >= row2 * (1 - no_prev) + no_prev * blk, col2 <= row2 + blk)
    lane = lax.broadcasted_iota(jnp.int32, (blk, LANES), 1)
    n_blk = q_ref.shape[0] // blk
    for rr in range(rps):
        base = rr * D_MODEL
        r = pid * rps + rr
        v_scr[rr, :blk, :] = vp_ref[:, base:base + D_MODEL]
        v_scr[rr, blk:, :] = vc_ref[:, base:base + D_MODEL]
        for h in range(DIL_HEADS):
            cols = slice(base + h * DIL_HEAD_DIM, base + (h + 1) * DIL_HEAD_DIM)
            kt_scr[rr, h, :, :blk] = kp_ref[:, cols].T
            for c in range(n_blk):
                kt_scr[rr, h, :, (c + 1) * blk:(c + 2) * blk] = kc_ref[c * blk:(c + 1) * blk, cols].T
        for c in range(n_blk):
            rows = slice(c * blk, (c + 1) * blk)
            win = slice(c * blk, (c + 2) * blk)
            tok = rows if d == 1 else pl.ds(c * blk * d + r, blk, stride=d)
            for h in range(DIL_HEADS):
                cols = slice(base + h * DIL_HEAD_DIM, base + (h + 1) * DIL_HEAD_DIM)
                s = jnp.dot(q_ref[rows, cols], kt_scr[rr, h, :, win], preferred_element_type=F32)
                s_scr[rr, c, h] = jnp.where(band_ok if c else first_band_ok, s, MASKED)
            s_all = s_scr[rr, c]
            m = jnp.max(s_all, axis=-1, keepdims=True)
            e = jnp.exp(s_all - m)
            l = jnp.sum(e, axis=-1, keepdims=True)
            p_scr[rr, c] = e.astype(BF16)
            inv = 1.0 / l
            lse = m + jnp.log(l)
            lse_tile = jnp.zeros((blk, LANES), F32)
            for h in range(DIL_HEADS):
                hcols = slice(h * DIL_HEAD_DIM, (h + 1) * DIL_HEAD_DIM)
                o = jnp.dot(p_scr[rr, c, h], v_scr[rr, win, hcols], preferred_element_type=F32) * inv[h]
                if d == 1:
                    o_ref[rows, hcols] = o.astype(BF16)
                else:
                    acc[0][h, tok, :] = o
                lse_tile = jnp.where(lane == h, lse[h], lse_tile)
            lse_ref[tok, :] = lse_tile

    if d > 1:
        @pl.when(pid == d // rps - 1)
        def _():
            for h in range(DIL_HEADS):
                o_ref[:, h * DIL_HEAD_DIM:(h + 1) * DIL_HEAD_DIM] = acc[0][h].astype(BF16)


def _dilated_attention(q, k, v, dilation, *, batch, seq, tq, rps):
    d = dilation
    n_sub = seq // d
    nq = n_sub // tq
    per = tq // DIL_BLK
    rows_out = tq * d
    width = rps * D_MODEL
    cur = pl.BlockSpec((None, tq, width), lambda b, i, r: (b, i, r))
    prev = pl.BlockSpec((None, DIL_BLK, width), lambda b, i, r: (b, jnp.maximum(i * per - 1, 0), r))
    scratch = [pltpu.VMEM((rps, per, DIL_HEADS, DIL_BLK, 2 * DIL_BLK), F32),
               pltpu.VMEM((rps, per, DIL_HEADS, DIL_BLK, 2 * DIL_BLK), BF16),
               pltpu.VMEM((rps, DIL_HEADS, DIL_HEAD_DIM, tq + DIL_BLK), BF16),
               pltpu.VMEM((rps, tq + DIL_BLK, D_MODEL), BF16)]
    if d > 1:
        scratch.append(pltpu.VMEM((DIL_HEADS, rows_out, DIL_HEAD_DIM), F32))
    return pl.pallas_call(
        functools.partial(_dil_attn_kernel, dilation=d, rps=rps),
        out_shape=(jax.ShapeDtypeStruct((batch * seq, D_MODEL), BF16),
                   jax.ShapeDtypeStruct((batch * seq, LANES), F32)),
        grid=(batch, nq, d // rps),
        in_specs=[cur, cur, cur, prev, prev],
        out_specs=(pl.BlockSpec((rows_out, D_MODEL), lambda b, i, r: (b * nq + i, 0)),
                   pl.BlockSpec((rows_out, LANES), lambda b, i, r: (b * nq + i, 0))),
        scratch_shapes=scratch,
        compiler_params=_params("parallel", "arbitrary", "arbitrary"),
        name=f"dil_attn_d{d}",
    )(q, k, v, k, v)


def _retention_tables(seq):
    half = RET_QK_DIM // 2
    angle = 1.0 / (RET_THETA ** jnp.linspace(0.0, 1.0, half, dtype=F32))
    ang = jnp.arange(seq, dtype=F32)[:, None] * angle[None]
    c_sz = RET_CHUNK
    log_g = jnp.log(1.0 - 2.0 ** (-5.0 - jnp.arange(RET_HEADS, dtype=F32)))
    idx = jnp.arange(c_sz, dtype=F32)
    diff = idx[:, None] - idx[None, :]
    decay = jnp.where(diff[None] >= 0, jnp.exp(jnp.maximum(diff, 0.0)[None] * log_g[:, None, None]), 0.0)
    xi = jnp.exp((idx[None] + 1.0) * log_g[:, None])
    zeta = jnp.exp((c_sz - 1.0 - idx[None]) * log_g[:, None])
    g_chunk = jnp.exp(c_sz * log_g)
    xi_b = jnp.broadcast_to(xi[:, :, None], (RET_HEADS, c_sz, RET_V_DIM))
    zeta_b = jnp.broadcast_to(zeta[:, :, None], (RET_HEADS, c_sz, RET_QK_DIM))
    gc_b = jnp.broadcast_to(g_chunk[:, None, None], (RET_HEADS, 1, RET_V_DIM))
    return jnp.cos(ang), jnp.sin(ang), decay, xi_b, zeta_b, gc_b


def _rope_tables(seq):
    half = ROT_DIMS // 2
    inv_freq = ROPE_THETA ** (-jnp.arange(0, ROT_DIMS, 2, dtype=F32) / ROT_DIMS)
    ang = jnp.arange(seq, dtype=F32)[:, None] * inv_freq[None]
    cos, sin = jnp.cos(ang), jnp.sin(ang)
    gap = LANES // 2 - half
    ones = jnp.ones((seq, gap), F32)
    zeros = jnp.zeros((seq, gap), F32)
    return (jnp.concatenate([cos, ones, cos, ones], axis=1),
            jnp.concatenate([-sin, zeros, sin, zeros], axis=1))


def _pair_split_cols(w, dim):
    lead, n = w.shape[:-1], w.shape[-1]
    return w.reshape(*lead, n // dim, dim // 2, 2).swapaxes(-1, -2).reshape(*lead, n)


def _rope_cols(w):
    k, n = w.shape
    half = ROT_DIMS // 2
    gap = LANES // 2 - half
    wh = w.reshape(k, n // DIL_HEAD_DIM, DIL_HEAD_DIM)
    wh = jnp.concatenate([wh[:, :, :half], wh[:, :, ROT_DIMS:ROT_DIMS + gap],
                          wh[:, :, half:ROT_DIMS], wh[:, :, ROT_DIMS + gap:]], axis=2)
    return wh.reshape(k, n)


def kernel(x, ret_w_in, ret_w_out, kv_w, dil_w_q, dil_w_out, mlp_w_up, mlp_w_down, ln_g, ln_b):
    batch, seq, _ = x.shape
    xf = x.reshape(batch * seq, D_MODEL)

    cos_r, sin_r, decay, xi_b, zeta_b, gc_b = _retention_tables(seq)
    dil_tm = 512
    rope_tabs = _rope_tables(seq)
    attn_tq = (1024, 512, 256)
    attn_rps = (1, 2, 2)

    n_qk = 2 * D_MODEL
    n_k = N_GROUPS * D_MODEL
    q_scale = DIL_HEAD_DIM ** -0.5
    k_units = ((True, 1.0),) * N_GROUPS
    v_units = ((False, 1.0),) * N_GROUPS
    q_units = ((True, q_scale),) * N_GROUPS

    w_in = ret_w_in.astype(BF16)
    w_in = jnp.concatenate([_pair_split_cols(w_in[:, :, :n_qk], RET_QK_DIM), w_in[:, :, n_qk:]], axis=2)
    kv_w_b = kv_w.astype(BF16)
    dil_w_q_b = dil_w_q.astype(BF16)
    ret_w_out_b = ret_w_out.astype(BF16)
    dil_w_out_b = dil_w_out.astype(BF16)
    w_up = mlp_w_up.astype(BF16)
    w_down = mlp_w_down.astype(BF16)
    ln = jnp.stack([ln_g[:, 0], ln_b[:, 0], ln_g[:, 1], ln_b[:, 1]], axis=1)

    ks = vs = qs = None
    for l in range(DEPTH):
        if l < N_A:
            proj = _ret_project(xf, w_in, l, cos_r, sin_r, seq=seq)
            mix = _retention(proj, decay, xi_b, zeta_b, gc_b, batch=batch, seq=seq)
            xf = _layer_tail([mix], [], xf, ret_w_out_b, l, ln[l], w_up, w_down, l)
        else:
            if l > N_A:
                w_q = _rope_cols(dil_w_q_b[l - N_A])
                qs = _dil_project(xf, w_q, rope_tabs, q_units, batch=batch, seq=seq, tm=2 * dil_tm)
            outs, lses = [], []
            for gi, d in enumerate(DILATIONS):
                o, lse = _dilated_attention(qs[gi], ks[gi], vs[gi], d, batch=batch, seq=seq,
                                            tq=attn_tq[gi], rps=attn_rps[gi])
                outs.append(o)
                lses.append(lse)
            xf = _layer_tail(outs, lses, xf, dil_w_out_b, l - N_A, ln[l], w_up, w_down, l)
        if l == N_A - 1:
            w_kvq = jnp.concatenate([_rope_cols(kv_w_b[:, :n_k]), kv_w_b[:, n_k:],
                                     _rope_cols(dil_w_q_b[0])], axis=1)
            kvq = _dil_project(xf, w_kvq, rope_tabs, k_units + v_units + q_units,
                               batch=batch, seq=seq, tm=dil_tm)
            ks, vs, qs = kvq[:N_GROUPS], kvq[N_GROUPS:2 * N_GROUPS], kvq[2 * N_GROUPS:]
    return xf.reshape(batch, seq, D_MODEL)
```
